```python
import math, functools
import jax
import jax.numpy as jnp
from jax import lax
import numpy as np

D_MODEL = 1024
BATCH = 16
SEQ = 2048
DEPTH = 2
DEC_BATCH = 32
DEC_SEQ = 4
PAST_LEN = 16384
PAGE_SIZE = 128

D_MIX = D_MODEL
H_A = 4
HD_A = D_MODEL // 16
W_A = H_A * 2 * HD_A
POOL_WINDOWS = (2, 4, 8, 16)
G_B = len(POOL_WINDOWS)
CG_B = D_MODEL // 16
W_B = G_B * CG_B
POOL_BUF = max(POOL_WINDOWS) - 1
H_C = 4
HD_C = D_MODEL // 16
W_C = H_C * HD_C
CHUNK = 128
D_IN = 3 * W_A + W_B + 2 * W_C
N_GROUPS = 4
EXP_PER_GROUP = 4
N_EXPERTS = N_GROUPS * EXP_PER_GROUP
TOP_K_INNER = 2
D_FF_EXP = D_MODEL // 2
QUERY_BLOCK = 128
NORM_EPS = 1e-6
SUBLN_EPS = 1e-5

kernel_name = 'hymba_diffattn_pool_sgu_hmoe_step'


def rms_norm(x, g, eps=NORM_EPS):
    xf = x.astype(jnp.float32)
    y = xf * lax.rsqrt(jnp.mean(xf * xf, axis=-1, keepdims=True) + eps)
    return (y * g.astype(jnp.float32)).astype(x.dtype)


def layer_norm(x, g, b, eps=NORM_EPS):
    xf = x.astype(jnp.float32)
    mu = jnp.mean(xf, axis=-1, keepdims=True)
    var = jnp.mean(jnp.square(xf - mu), axis=-1, keepdims=True)
    y = (xf - mu) * lax.rsqrt(var + eps)
    return (y * g.astype(jnp.float32) + b.astype(jnp.float32)).astype(x.dtype)


def lambda_init(layer):
    return 0.8 - 0.6 * math.exp(-0.3 * layer)


def diff_lambda(lq1, lk1, lq2, lk2, li):
    f = jnp.float32
    return (jnp.exp(jnp.sum(lq1.astype(f) * lk1.astype(f)))
            - jnp.exp(jnp.sum(lq2.astype(f) * lk2.astype(f))) + li)


def split_proj(z):
    B, T = z.shape[0], z.shape[1]
    o_b = 3 * W_A
    o_u = o_b + W_B
    o_g = o_u + W_C
    q = z[..., :W_A].reshape(B, T, H_A, 2, HD_A)
    k = z[..., W_A:2 * W_A].reshape(B, T, H_A, 2 * HD_A)
    v = z[..., 2 * W_A:o_b].reshape(B, T, H_A, 2 * HD_A)
    return q, k, v, z[..., o_b:o_u], z[..., o_u:o_g], z[..., o_g:]


def diff_softmax_mix(q, k, v, mask, lam):
    s = jnp.einsum('bqhmd,bkhmd->bhmqk', q, k, preferred_element_type=jnp.float32) * (HD_A ** -0.5)
    s = jnp.where(mask, s, -jnp.inf)
    p = jax.nn.softmax(s, axis=-1)
    a = p[:, :, 0] - lam * p[:, :, 1]
    return jnp.einsum('bhqk,bkhe->bqhe', a.astype(v.dtype), v)


def prompt_diff_attention(q, k, v, lam):
    B, T = q.shape[0], q.shape[1]
    nb = T // QUERY_BLOCK
    k4 = k.reshape(B, T, H_A, 2, HD_A)
    qb = jnp.swapaxes(q.reshape(B, nb, QUERY_BLOCK, H_A, 2, HD_A), 0, 1)
    kpos = jnp.arange(T)

    def one_block(args):
        q_blk, blk = args
        qpos = blk * QUERY_BLOCK + jnp.arange(QUERY_BLOCK)
        return diff_softmax_mix(q_blk, k4, v, kpos[None, :] <= qpos[:, None], lam)

    o = lax.map(one_block, (qb, jnp.arange(nb)))
    return jnp.swapaxes(o, 0, 1).reshape(B, T, H_A, 2 * HD_A)


def sample_diff_attention(q, k, v, lam, cache_k, cache_v, page_table, layer):
    Tn = q.shape[1]
    past = page_table.shape[1] * PAGE_SIZE
    kpos = jnp.arange(past + Tn)
    qpos = past + jnp.arange(Tn)
    mask = kpos[None, :] <= qpos[:, None]

    def one_seq(args):
        q_s, k_s, v_s, pt = args
        k_past = cache_k[layer, pt].reshape(past, H_A, 2 * HD_A)
        v_past = cache_v[layer, pt].reshape(past, H_A, 2 * HD_A)
        k_all = jnp.concatenate([k_past, k_s], axis=0).reshape(past + Tn, H_A, 2, HD_A)
        v_all = jnp.concatenate([v_past, v_s], axis=0)
        return diff_softmax_mix(q_s[None], k_all[None], v_all[None], mask, lam)[0]

    return lax.map(one_seq, (q, k, v, page_table))


def diff_attn_out(o, subln_g, li):
    B, T = o.shape[0], o.shape[1]
    o = rms_norm(o, subln_g, SUBLN_EPS) * (1.0 - li)
    return o.reshape(B, T, W_A)


def pool_mixer(z_ext, pos0, w_pool, scale):
    B, L = z_ext.shape[0], z_ext.shape[1]
    T = L - POOL_BUF
    c = jnp.cumsum(z_ext.astype(jnp.float32), axis=1)
    c = jnp.concatenate([jnp.zeros((B, 1, W_B), jnp.float32), c], axis=1)
    end = POOL_BUF + 1 + jnp.arange(T)
    pos = pos0 + jnp.arange(T)
    means = []
    for gi, w in enumerate(POOL_WINDOWS):
        cs = c[..., gi * CG_B:(gi + 1) * CG_B]
        cnt = jnp.minimum(pos + 1, w).astype(jnp.float32)
        means.append((cs[:, end] - cs[:, end - w]) / cnt[None, :, None])
    d = jnp.concatenate(means, axis=-1) - z_ext[:, POOL_BUF:].astype(jnp.float32)
    d = d.reshape(B, T, G_B, CG_B).astype(z_ext.dtype)
    out = jnp.einsum('btgc,gce->btge', d, w_pool).reshape(B, T, W_B)
    return out * scale, z_ext[:, L - POOL_BUF:]


def spatial_gating(u, vn, w_s, b_s, chunk_len):
    B, T = u.shape[0], u.shape[1]
    n_chunks = T // chunk_len
    v5 = vn.reshape(B, n_chunks, chunk_len, H_C, HD_C)
    w = jnp.tril(w_s[:, :chunk_len, :chunk_len])
    bias = jnp.swapaxes(b_s[:, :chunk_len], 0, 1)[None, None, :, :, None]
    mixed = jnp.einsum('hij,bnjhd->bnihd', w, v5) + bias
    return u * mixed.reshape(B, T, W_C)


def hier_moe(x, w_rg, b_rg, w_re, b_re, w_g, w_u, w_d):
    shp = x.shape
    xt = x.reshape(-1, shp[-1])
    f32 = jnp.float32
    g_logits = (xt @ w_rg).astype(f32) + b_rg.astype(f32)
    g_prob = jax.nn.softmax(g_logits, axis=-1)
    _, g_idx = lax.top_k(g_logits, 1)
    g_w = jnp.take_along_axis(g_prob, g_idx, axis=1)
    e_logits = ((xt @ w_re).astype(f32) + b_re.astype(f32)).reshape(-1, N_GROUPS, EXP_PER_GROUP)
    e_in = jnp.take_along_axis(e_logits, g_idx[:, :, None], axis=1)[:, 0]
    e_top, e_idx = lax.top_k(e_in, TOP_K_INNER)
    e_w = jax.nn.softmax(e_top, axis=-1) * g_w
    eid = g_idx * EXP_PER_GROUP + e_idx
    gates = jnp.einsum('nk,nke->ne', e_w, jax.nn.one_hot(eid, N_EXPERTS, dtype=f32)).astype(x.dtype)
    y = jnp.zeros_like(xt)
    for e in range(N_EXPERTS):
        h = jax.nn.silu(xt @ w_g[e]) * (xt @ w_u[e])
        y = y + gates[:, e:e + 1] * (h @ w_d[e])
    return y.reshape(shp)


def mixer_sublayer(x, attend, pool_prev, pos0, chunk_len, norm_g, w_in, w_out, subln_g, li,
                   w_pool, pool_scale, sgu_g, sgu_b, w_s, b_s):
    z = rms_norm(x, norm_g) @ w_in
    q, k, v, p, u, g = split_proj(z)
    a_out = diff_attn_out(attend(q, k, v), subln_g, li)
    b_out, pool_state = pool_mixer(jnp.concatenate([pool_prev.astype(p.dtype), p], axis=1), pos0, w_pool, pool_scale)
    vn = layer_norm(jax.nn.gelu(g, approximate=False), sgu_g, sgu_b)
    c_out = spatial_gating(jax.nn.gelu(u, approximate=False), vn, w_s, b_s, chunk_len)
    y = x + jnp.concatenate([a_out, b_out, c_out], axis=-1) @ w_out
    return y, k, v, pool_state, vn


def setup_inputs(seed: int = 0) -> dict:
    key = jax.random.key(seed)
    ks = jax.random.split(key, 32)
    f32 = jnp.float32

    def nrm(k, shape, s):
        return jax.random.normal(k, shape, f32) * s

    n_pages = PAST_LEN // PAGE_SIZE
    n_used = DEC_BATCH * n_pages
    n_pool = n_used + max(n_used // 4, 1)
    perm = jax.random.permutation(ks[5], n_pool)
    page_table = perm[:n_used].reshape(DEC_BATCH, n_pages).astype(jnp.int32)
    return {
        'x_prompt': nrm(ks[0], (BATCH, SEQ, D_MODEL), 1.0),
        'x_sample': nrm(ks[1], (DEC_BATCH, DEC_SEQ, D_MODEL), 1.0),
        'cache_k': nrm(ks[2], (DEPTH, n_pool, PAGE_SIZE, H_A, 2 * HD_A), 1.0),
        'cache_v': nrm(ks[3], (DEPTH, n_pool, PAGE_SIZE, H_A, 2 * HD_A), 1.0),
        'state_pool': nrm(ks[4], (DEPTH, DEC_BATCH, POOL_BUF, W_B), 1.0),
        'page_table': page_table,
        'norm_mix_g': 1.0 + nrm(ks[6], (DEPTH, D_MODEL), 0.02),
        'norm_ffn_g': 1.0 + nrm(ks[7], (DEPTH, D_MODEL), 0.02),
        'w_in': nrm(ks[8], (DEPTH, D_MODEL, D_IN), D_MODEL ** -0.5),
        'w_out': nrm(ks[9], (DEPTH, D_MIX, D_MODEL), D_MIX ** -0.5),
        'lambda_q1': nrm(ks[10], (DEPTH, HD_A), 0.1),
        'lambda_k1': nrm(ks[11], (DEPTH, HD_A), 0.1),
        'lambda_q2': nrm(ks[12], (DEPTH, HD_A), 0.1),
        'lambda_k2': nrm(ks[13], (DEPTH, HD_A), 0.1),
        'subln_g': 1.0 + nrm(ks[14], (DEPTH, 2 * HD_A), 0.02),
        'w_pool': nrm(ks[15], (DEPTH, G_B, CG_B, CG_B), CG_B ** -0.5),
        'pool_scale': 1.0 + nrm(ks[16], (DEPTH, W_B), 0.02),
        'sgu_norm_g': 1.0 + nrm(ks[17], (DEPTH, W_C), 0.02),
        'sgu_norm_b': nrm(ks[18], (DEPTH, W_C), 0.02),
        'w_spatial': nrm(ks[19], (DEPTH, H_C, CHUNK, CHUNK), CHUNK ** -0.5),
        'b_spatial': 1.0 + nrm(ks[20], (DEPTH, H_C, CHUNK), 0.02),
        'w_router_group': nrm(ks[21], (DEPTH, D_MODEL, N_GROUPS), D_MODEL ** -0.5),
        'b_router_group': nrm(ks[22], (DEPTH, N_GROUPS), 0.01),
        'w_router_expert': nrm(ks[23], (DEPTH, D_MODEL, N_EXPERTS), D_MODEL ** -0.5),
        'b_router_expert': nrm(ks[24], (DEPTH, N_EXPERTS), 0.01),
        'w_exp_gate': nrm(ks[25], (DEPTH, N_EXPERTS, D_MODEL, D_FF_EXP), D_MODEL ** -0.5),
        'w_exp_up': nrm(ks[26], (DEPTH, N_EXPERTS, D_MODEL, D_FF_EXP), D_MODEL ** -0.5),
        'w_exp_down': nrm(ks[27], (DEPTH, N_EXPERTS, D_FF_EXP, D_MODEL), D_FF_EXP ** -0.5),
        'final_norm_g': 1.0 + nrm(ks[28], (D_MODEL,), 0.02),
    }


def reference(x_prompt, x_sample, cache_k, cache_v, state_pool, page_table,
              norm_mix_g, norm_ffn_g, w_in, w_out,
              lambda_q1, lambda_k1, lambda_q2, lambda_k2, subln_g,
              w_pool, pool_scale, sgu_norm_g, sgu_norm_b, w_spatial, b_spatial,
              w_router_group, b_router_group, w_router_expert, b_router_expert,
              w_exp_gate, w_exp_up, w_exp_down, final_norm_g):
    xp, xs = x_prompt, x_sample
    B = xp.shape[0]
    Ts = xs.shape[1]
    past = page_table.shape[1] * PAGE_SIZE
    kp_l, vp_l, pp_l, ks_l, vs_l, ps_l, cs_l = [], [], [], [], [], [], []
    for l in range(DEPTH):
        li = lambda_init(l)
        lam = diff_lambda(lambda_q1[l], lambda_k1[l], lambda_q2[l], lambda_k2[l], li)
        shared = (norm_mix_g[l], w_in[l], w_out[l], subln_g[l], li, w_pool[l], pool_scale[l],
                  sgu_norm_g[l], sgu_norm_b[l], w_spatial[l], b_spatial[l])
        moe_w = (w_router_group[l], b_router_group[l], w_router_expert[l], b_router_expert[l],
                 w_exp_gate[l], w_exp_up[l], w_exp_down[l])
        xp, k_p, v_p, pool_p, _ = mixer_sublayer(
            xp, functools.partial(prompt_diff_attention, lam=lam),
            jnp.zeros((B, POOL_BUF, W_B), xp.dtype), 0, CHUNK, *shared)
        xp = xp + hier_moe(rms_norm(xp, norm_ffn_g[l]), *moe_w)
        attend_s = functools.partial(sample_diff_attention, lam=lam, cache_k=cache_k, cache_v=cache_v,
                                     page_table=page_table, layer=l)
        xs, k_s, v_s, pool_s, sgu_v = mixer_sublayer(
            xs, attend_s, state_pool[l], past, Ts, *shared)
        xs = xs + hier_moe(rms_norm(xs, norm_ffn_g[l]), *moe_w)
        kp_l.append(k_p); vp_l.append(v_p); pp_l.append(pool_p)
        ks_l.append(k_s); vs_l.append(v_s); ps_l.append(pool_s); cs_l.append(sgu_v)
    y_prompt = rms_norm(xp, final_norm_g)
    y_sample = rms_norm(xs, final_norm_g)
    return (y_prompt, y_sample, jnp.stack(kp_l), jnp.stack(vp_l), jnp.stack(pp_l),
            jnp.stack(ks_l), jnp.stack(vs_l), jnp.stack(ps_l), jnp.stack(cs_l))
```

```python
import functools
import math

import jax
import jax.numpy as jnp
from jax import lax
from jax.experimental import pallas as pl
from jax.experimental.pallas import tpu as pltpu

F32 = jnp.float32
BF16 = jnp.bfloat16

D_MODEL = 1024
H_A = 4
HD_A = 64
W_A = H_A * 2 * HD_A
POOL_WINDOWS = (2, 4, 8, 16)
CG_B = 64
W_B = len(POOL_WINDOWS) * CG_B
POOL_BUF = max(POOL_WINDOWS) - 1
H_C = 4
HD_C = 64
W_C = H_C * HD_C
CHUNK = 128
D_IN = 3 * W_A + W_B + 2 * W_C
N_GROUPS = 4
EXP_PER_GROUP = 4
N_EXPERTS = N_GROUPS * EXP_PER_GROUP
N_PAIRS = 6
N_BUCKETS = N_GROUPS * N_PAIRS
D_FF = D_MODEL // 2
PAGE_SIZE = 128
NORM_EPS = 1e-6
SUBLN_EPS = 1e-5

LANES = 128
POOL_HALO = 16
VMEM_LIMIT = 56 * 1024 * 1024

TM_INPROJ = 512
TQ_ATTN = 256
TM_MIX = 256
TM_MOE = 256
TM_MOE_SAMPLE = 16
TM_FINAL = 512
PAGES_PER_STEP = 8


def _lambda_init(layer):
    return 0.8 - 0.6 * math.exp(-0.3 * layer)


def _cparams(sem):
    return pltpu.CompilerParams(dimension_semantics=sem, vmem_limit_bytes=VMEM_LIMIT)


def _full(shape):
    return pl.BlockSpec(shape, lambda *_: (0,) * len(shape))


def _rms(x, g, eps):
    return x * lax.rsqrt(jnp.mean(x * x, axis=-1, keepdims=True) + eps) * g


def _gelu(x):
    return 0.5 * x * (1.0 + lax.erf(x * (2.0 ** -0.5)))


def _inproj_body(*refs, has_moe):
    if has_moe:
        x_ref, y_ref, g_ref, w_ref, xs_ref, q_ref, k_ref, v_ref, kb_ref, vb_ref, p_ref, u_ref, gv_ref = refs
        x = x_ref[...] + y_ref[...]
        xs_ref[...] = x
    else:
        x_ref, g_ref, w_ref, q_ref, k_ref, v_ref, kb_ref, vb_ref, p_ref, u_ref, gv_ref = refs
        x = x_ref[...]
    xn = _rms(x, g_ref[...], NORM_EPS).astype(BF16)

    def seg(lo, hi):
        return jnp.dot(xn, w_ref[:, lo:hi], preferred_element_type=F32)

    q_ref[...] = (seg(0, W_A) * (HD_A ** -0.5)).astype(BF16)
    k = seg(W_A, 2 * W_A)
    k_ref[...] = k
    kb_ref[...] = k.astype(BF16)
    v = seg(2 * W_A, 3 * W_A)
    v_ref[...] = v
    vb_ref[...] = v.astype(BF16)
    o = 3 * W_A
    p_ref[...] = seg(o, o + W_B)
    u_ref[...] = seg(o + W_B, o + W_B + W_C)
    gv_ref[...] = seg(o + W_B + W_C, D_IN)


def _inproj(x, y_moe, g, w_bf, tm):
    n = x.shape[0]
    has_moe = y_moe is not None
    row = lambda w: pl.BlockSpec((tm, w), lambda i: (i, 0))
    in_specs = [row(D_MODEL)] + ([row(D_MODEL)] if has_moe else []) + [_full((1, D_MODEL)), _full((D_MODEL, D_IN))]
    outs = [
        (jax.ShapeDtypeStruct((n, W_A), BF16), row(W_A)),
        (jax.ShapeDtypeStruct((n, W_A), F32), row(W_A)),
        (jax.ShapeDtypeStruct((n, W_A), F32), row(W_A)),
        (jax.ShapeDtypeStruct((n, W_A), BF16), row(W_A)),
        (jax.ShapeDtypeStruct((n, W_A), BF16), row(W_A)),
        (jax.ShapeDtypeStruct((n, W_B), F32), row(W_B)),
        (jax.ShapeDtypeStruct((n, W_C), F32), row(W_C)),
        (jax.ShapeDtypeStruct((n, W_C), F32), row(W_C)),
    ]
    if has_moe:
        outs = [(jax.ShapeDtypeStruct((n, D_MODEL), F32), row(D_MODEL))] + outs
    args = (x, y_moe) if has_moe else (x,)
    res = pl.pallas_call(
        functools.partial(_inproj_body, has_moe=has_moe),
        grid=(n // tm,),
        in_specs=in_specs,
        out_specs=[s for _, s in outs],
        out_shape=[s for s, _ in outs],
        compiler_params=_cparams(("parallel",)),
        name="inproj",
    )(*args, g.reshape(1, D_MODEL), w_bf)
    if has_moe:
        return res[0], res[1:]
    return x, res


def _diff_lambda(lq1, lk1, lq2, lk2, li):
    s1 = jnp.sum(lq1[...] * lk1[...], axis=-1, keepdims=True)
    s2 = jnp.sum(lq2[...] * lk2[...], axis=-1, keepdims=True)
    return jnp.exp(s1) - jnp.exp(s2) + li


def _split_maps(q):
    lane = lax.broadcasted_iota(jnp.int32, q.shape, 1)
    zero = jnp.zeros_like(q)
    return jnp.where(lane < HD_A, q, zero), jnp.where(lane >= HD_A, q, zero)


def _qk(q, k):
    return lax.dot_general(q, k, (((1,), (1,)), ((), ())), preferred_element_type=F32)


def _attn_finish(o1, o2, lam, sg, li):
    o = o1 - lam * o2
    return _rms(o, sg, SUBLN_EPS) * (1.0 - li)


def _attn_prompt_body(q_ref, k_ref, v_ref, lq1, lk1, lq2, lk2, sg_ref, o_ref, m_sc, l_sc, a_sc, *, tq, li):
    i = pl.program_id(2)
    qz = _split_maps(q_ref[...])
    m_sc[...] = jnp.full(m_sc.shape, -jnp.inf, F32)
    l_sc[...] = jnp.zeros(l_sc.shape, F32)
    a_sc[...] = jnp.zeros(a_sc.shape, F32)

    def step(j, masked):
        start = pl.multiple_of(j * tq, tq)
        kb = k_ref[pl.ds(start, tq), :]
        vb = v_ref[pl.ds(start, tq), :]
        for m in range(2):
            s = _qk(qz[m], kb)
            if masked:
                r = lax.broadcasted_iota(jnp.int32, s.shape, 0)
                c = lax.broadcasted_iota(jnp.int32, s.shape, 1)
                s = jnp.where(c <= r, s, -jnp.inf)
            m_old = m_sc[m]
            m_new = jnp.maximum(m_old, jnp.max(s, axis=-1, keepdims=True))
            alpha = jnp.exp(m_old - m_new)
            p = jnp.exp(s - m_new)
            l_sc[m] = alpha * l_sc[m] + jnp.sum(p, axis=-1, keepdims=True)
            a_sc[m] = alpha * a_sc[m] + jnp.dot(p.astype(BF16), vb, preferred_element_type=F32)
            m_sc[m] = m_new

    def body(j, carry):
        step(j, False)
        return carry

    lax.fori_loop(0, i, body, 0)
    step(i, True)

    lam = _diff_lambda(lq1, lk1, lq2, lk2, li)
    o1 = a_sc[0] / l_sc[0]
    o2 = a_sc[1] / l_sc[1]
    o_ref[...] = _attn_finish(o1, o2, lam, sg_ref[...], li).astype(BF16)


def _attn_prompt(q, kb, vb, lam_params, sg, li, batch, seq):
    tq = TQ_ATTN
    nq = seq // tq
    blk = pl.BlockSpec((tq, LANES), lambda b, h, i: (b * nq + i, h))
    kv = pl.BlockSpec((seq, LANES), lambda b, h, i: (b, h))
    return pl.pallas_call(
        functools.partial(_attn_prompt_body, tq=tq, li=li),
        grid=(batch, H_A, nq),
        in_specs=[blk, kv, kv] + [_full((1, HD_A))] * 4 + [_full((1, 2 * HD_A))],
        out_specs=blk,
        out_shape=jax.ShapeDtypeStruct((batch * seq, W_A), BF16),
        scratch_shapes=[pltpu.VMEM((2, tq, 1), F32), pltpu.VMEM((2, tq, 1), F32), pltpu.VMEM((2, tq, LANES), F32)],
        compiler_params=_cparams(("parallel", "parallel", "arbitrary")),
        name="attn_prompt",
    )(q, kb, vb, *lam_params, sg)


def _attn_sample_body(pt_ref, q_ref, kn_ref, vn_ref, *rest, n_pages_step, n_tok, li):
    kp = rest[:n_pages_step]
    vp = rest[n_pages_step:2 * n_pages_step]
    lq1, lk1, lq2, lk2, sg_ref, o_ref, m_sc, l_sc, a_sc = rest[2 * n_pages_step:]
    c = pl.program_id(1)
    nq = n_tok * H_A

    @pl.when(c == 0)
    def _():
        m_sc[...] = jnp.full(m_sc.shape, -jnp.inf, F32)
        l_sc[...] = jnp.zeros(l_sc.shape, F32)
        a_sc[...] = jnp.zeros(a_sc.shape, F32)

    q1, q2 = _split_maps(q_ref[...])
    qb = jnp.concatenate([q1, q2], axis=0)

    def accumulate(blocks):
        ss = []
        for k, _, mask in blocks:
            ss.append(jnp.where(mask, _qk(qb, k.astype(BF16)), -jnp.inf))
        m_old = m_sc[...]
        m_new = m_old
        for s in ss:
            m_new = jnp.maximum(m_new, jnp.max(s, axis=-1, keepdims=True))
        alpha = jnp.exp(m_old - m_new)
        l_new = alpha * l_sc[...]
        a_new = alpha * a_sc[...]
        for s, (_, v, _) in zip(ss, blocks):
            p = jnp.exp(s - m_new)
            l_new = l_new + jnp.sum(p, axis=-1, keepdims=True)
            a_new = a_new + jnp.dot(p.astype(BF16), v.astype(BF16), preferred_element_type=F32)
        m_sc[...] = m_new
        l_sc[...] = l_new
        a_sc[...] = a_new

    n_rows = PAGE_SIZE * H_A
    r = lax.broadcasted_iota(jnp.int32, (2 * nq, n_rows), 0)
    col = lax.broadcasted_iota(jnp.int32, (2 * nq, n_rows), 1)
    head_of = lambda t: t & (H_A - 1)
    head_mask = head_of(r) == head_of(col)
    accumulate([(kp[j][...], vp[j][...], head_mask) for j in range(n_pages_step)])

    @pl.when(c == pl.num_programs(1) - 1)
    def _():
        rn = lax.broadcasted_iota(jnp.int32, (2 * nq, nq), 0)
        cn = lax.broadcasted_iota(jnp.int32, (2 * nq, nq), 1)
        q_row = jnp.where(rn >= nq, rn - nq, rn)
        new_mask = (head_of(rn) == head_of(cn)) & (cn <= q_row)
        accumulate([(kn_ref[...], vn_ref[...], new_mask)])
        lam = _diff_lambda(lq1, lk1, lq2, lk2, li)
        o = a_sc[...] / l_sc[...]
        o_ref[...] = _attn_finish(o[:nq], o[nq:], lam, sg_ref[...], li).astype(BF16)


def _attn_sample(q, k_new, v_new, cache_k, cache_v, page_table, layer, lam_params, sg, li, n_tok):
    n_seq, n_pages = page_table.shape
    pps = PAGES_PER_STEP
    assert n_pages % pps == 0
    nq = n_tok * H_A
    n_rows = PAGE_SIZE * H_A

    def page_spec(j):
        return pl.BlockSpec((None, None, n_rows, LANES), lambda s, c, pt: (layer, pt[s, c * pps + j], 0, 0))

    seq_blk = pl.BlockSpec((None, nq, LANES), lambda s, c, pt: (s, 0, 0))
    const = lambda shape: pl.BlockSpec(shape, lambda s, c, pt: (0,) * len(shape))
    grid_spec = pltpu.PrefetchScalarGridSpec(
        num_scalar_prefetch=1,
        grid=(n_seq, n_pages // pps),
        in_specs=[seq_blk, seq_blk, seq_blk] + [page_spec(j) for j in range(pps)] * 2
        + [const((1, HD_A))] * 4 + [const((1, 2 * HD_A))],
        out_specs=seq_blk,
        scratch_shapes=[pltpu.VMEM((2 * nq, 1), F32), pltpu.VMEM((2 * nq, 1), F32), pltpu.VMEM((2 * nq, LANES), F32)],
    )
    return pl.pallas_call(
        functools.partial(_attn_sample_body, n_pages_step=pps, n_tok=n_tok, li=li),
        grid_spec=grid_spec,
        out_shape=jax.ShapeDtypeStruct((n_seq, nq, LANES), BF16),
        compiler_params=_cparams(("parallel", "arbitrary")),
        name="attn_sample",
    )(page_table, q, k_new, v_new, *([cache_k] * pps), *([cache_v] * pps), *lam_params, sg)


def _group_select(vals):
    lane = lax.broadcasted_iota(jnp.int32, vals[0].shape, 1)
    out = vals[-1]
    for g in range(len(vals) - 2, -1, -1):
        out = jnp.where(lane < (g + 1) * CG_B, vals[g], out)
    return out


def _router(xn, wr_hi, wr_lo, br):
    hi = xn.astype(BF16)
    lo = (xn - hi.astype(F32)).astype(BF16)
    logits = (jnp.dot(hi, wr_hi[...], preferred_element_type=F32)
              + jnp.dot(lo, wr_hi[...], preferred_element_type=F32)
              + jnp.dot(hi, wr_lo[...], preferred_element_type=F32)) + br[...]
    lane = lax.broadcasted_iota(jnp.int32, logits.shape, 1).astype(F32)
    neg = -jnp.inf
    big = float(LANES)

    def first_argmax(vals):
        top = jnp.max(vals, axis=-1, keepdims=True)
        idx = jnp.min(jnp.where(vals == top, lane, big), axis=-1, keepdims=True)
        return top, idx

    is_group = lane < N_GROUPS
    g_top, g_idx = first_argmax(jnp.where(is_group, logits, neg))
    g_w = 1.0 / jnp.sum(jnp.where(is_group, jnp.exp(logits - g_top), 0.0), axis=-1, keepdims=True)
    e_base = N_GROUPS + EXP_PER_GROUP * g_idx
    in_group = (lane >= e_base) & (lane < e_base + EXP_PER_GROUP)
    e_vals = jnp.where(in_group, logits, neg)
    e1, i1 = first_argmax(e_vals)
    e2, i2 = first_argmax(jnp.where(lane == i1, neg, e_vals))
    t = jnp.exp(e2 - e1)
    w1 = g_w / (1.0 + t)
    w2 = g_w * t / (1.0 + t)
    a1 = i1 - e_base
    a2 = i2 - e_base
    lo_id = jnp.minimum(a1, a2)
    hi_id = jnp.maximum(a1, a2)
    w_lo = jnp.where(a1 < a2, w1, w2)
    w_hi = jnp.where(a1 < a2, w2, w1)
    pair_base = jnp.where(lo_id == 0.0, 0.0, jnp.where(lo_id == 1.0, 3.0, 5.0))
    bucket = g_idx * N_PAIRS + pair_base + (hi_id - lo_id - 1.0)
    return jnp.where(lane == 0, bucket, jnp.where(lane == 1, w_lo, jnp.where(lane == 2, w_hi, 0.0)))


def _mix_tail(a, b_out, c_out, x, wout_ref, gffn_ref, wr_hi, wr_lo, br, y_ref, xn_ref, route_ref):
    y = x + (jnp.dot(a, wout_ref[0:W_A, :], preferred_element_type=F32)
             + jnp.dot(b_out.astype(BF16), wout_ref[W_A:W_A + W_B, :], preferred_element_type=F32)
             + jnp.dot(c_out.astype(BF16), wout_ref[W_A + W_B:, :], preferred_element_type=F32))
    y_ref[...] = y
    xn = _rms(y, gffn_ref[...], NORM_EPS)
    xn_ref[...] = xn.astype(BF16)
    route_ref[...] = _router(xn, wr_hi, wr_lo, br)


def _sgu_norm(gv, sgu_g, sgu_b):
    gg = _gelu(gv)
    mu = jnp.mean(gg, axis=-1, keepdims=True)
    d = gg - mu
    var = jnp.mean(d * d, axis=-1, keepdims=True)
    return d * lax.rsqrt(var + NORM_EPS) * sgu_g + sgu_b


def _mix_prompt_body(a_ref, p_ref, halo_ref, u_ref, gv_ref, x_ref, wpool_ref, pscale_ref, sgug_ref, sgub_ref,
                     ws_ref, bs_ref, wout_ref, gffn_ref, wr_hi, wr_lo, br,
                     y_ref, xn_ref, route_ref, ext_sc, *, tm, tiles_per_seq):
    i = pl.program_id(0)
    t_in_seq = i % tiles_per_seq
    p = p_ref[...]
    ext_sc[0:POOL_HALO, :] = jnp.where(t_in_seq == 0, 0.0, halo_ref[...])
    ext_sc[POOL_HALO:, :] = p
    pos = t_in_seq * tm + lax.broadcasted_iota(jnp.int32, (tm, 1), 0)
    acc = p
    means = []
    for k in range(1, max(POOL_WINDOWS)):
        acc = acc + ext_sc[pl.ds(POOL_HALO - k, tm), :]
        if k + 1 in POOL_WINDOWS:
            cnt = jnp.minimum(pos + 1, k + 1).astype(F32)
            means.append(acc / cnt)
    d = (_group_select(means) - p).astype(BF16)
    b_out = jnp.dot(d, wpool_ref[...], preferred_element_type=F32) * pscale_ref[...]

    vn = _sgu_norm(gv_ref[...], sgug_ref[...], sgub_ref[...]).astype(BF16)
    r = lax.broadcasted_iota(jnp.int32, (CHUNK, CHUNK), 0)
    c = lax.broadcasted_iota(jnp.int32, (CHUNK, CHUNK), 1)
    w_tril = [jnp.where(r >= c, ws_ref[h], 0.0).astype(BF16) for h in range(H_C)]
    chunks = []
    for ci in range(tm // CHUNK):
        vc = vn[ci * CHUNK:(ci + 1) * CHUNK, :]
        per_head = [jnp.dot(w_tril[h], vc, preferred_element_type=F32) for h in range(H_C)]
        chunks.append(_group_select(per_head) + bs_ref[...])
    mixed = jnp.concatenate(chunks, axis=0)
    c_out = _gelu(u_ref[...]) * mixed

    _mix_tail(a_ref[...], b_out, c_out, x_ref[...], wout_ref, gffn_ref, wr_hi, wr_lo, br, y_ref, xn_ref, route_ref)


def _mix_prompt(a, p, u, gv, x, lw, seq):
    n = x.shape[0]
    tm = TM_MIX
    tiles_per_seq = seq // tm
    halo_per_tile = tm // POOL_HALO
    row = lambda w: pl.BlockSpec((tm, w), lambda i: (i, 0))
    halo = pl.BlockSpec((POOL_HALO, W_B), lambda i: (jnp.maximum(i * halo_per_tile - 1, 0), 0))
    return pl.pallas_call(
        functools.partial(_mix_prompt_body, tm=tm, tiles_per_seq=tiles_per_seq),
        grid=(n // tm,),
        in_specs=[row(W_A), row(W_B), halo, row(W_C), row(W_C), row(D_MODEL),
                  _full((W_B, W_B)), _full((1, W_B)), _full((1, W_C)), _full((1, W_C)),
                  _full((H_C, CHUNK, CHUNK)), _full((CHUNK, W_C)), _full((D_MODEL, D_MODEL)),
                  _full((1, D_MODEL)), _full((D_MODEL, LANES)), _full((D_MODEL, LANES)), _full((1, LANES))],
        out_specs=[row(D_MODEL), row(D_MODEL), row(LANES)],
        out_shape=[jax.ShapeDtypeStruct((n, D_MODEL), F32), jax.ShapeDtypeStruct((n, D_MODEL), BF16),
                   jax.ShapeDtypeStruct((n, LANES), F32)],
        scratch_shapes=[pltpu.VMEM((POOL_HALO + tm, W_B), F32)],
        compiler_params=_cparams(("parallel",)),
        name="mix_prompt",
    )(a, p, p, u, gv, x, lw["wpool"], lw["pscale"], lw["sgu_g"], lw["sgu_b"], lw["w_s"], lw["bs_chunk"],
      lw["w_out"], lw["g_ffn"], lw["wr_hi"], lw["wr_lo"], lw["br"])


def _mix_sample_body(a_ref, pext_ref, u_ref, gv_ref, x_ref, wpool_ref, pscale_ref, sgug_ref, sgub_ref,
                     wl_ref, bl_ref, wout_ref, gffn_ref, wr_hi, wr_lo, br,
                     y_ref, xn_ref, route_ref, vn_ref, *, n_tok, n_seq):
    ds = []
    for i in range(n_tok):
        cur = pext_ref[POOL_BUF + i]
        acc = cur
        means = []
        for k in range(1, max(POOL_WINDOWS)):
            acc = acc + pext_ref[POOL_BUF + i - k]
            if k + 1 in POOL_WINDOWS:
                means.append(acc / float(k + 1))
        ds.append(_group_select(means) - cur)
    d = jnp.concatenate(ds, axis=0).astype(BF16)
    b_out = jnp.dot(d, wpool_ref[...], preferred_element_type=F32) * pscale_ref[...]

    vn = _sgu_norm(gv_ref[...], sgug_ref[...], sgub_ref[...])
    vn_ref[...] = vn
    rows = []
    for i in range(n_tok):
        acc = jnp.broadcast_to(bl_ref[i:i + 1, :], (n_seq, W_C))
        for j in range(i + 1):
            acc = acc + wl_ref[i * n_tok + j:i * n_tok + j + 1, :] * vn[j * n_seq:(j + 1) * n_seq, :]
        rows.append(acc)
    c_out = _gelu(u_ref[...]) * jnp.concatenate(rows, axis=0)

    _mix_tail(a_ref[...], b_out, c_out, x_ref[...], wout_ref, gffn_ref, wr_hi, wr_lo, br, y_ref, xn_ref, route_ref)


def _mix_sample(a, pext, u, gv, x, lw, n_tok, n_seq):
    n = x.shape[0]
    return pl.pallas_call(
        functools.partial(_mix_sample_body, n_tok=n_tok, n_seq=n_seq),
        grid=(1,),
        in_specs=[_full((n, W_A)), _full((POOL_BUF + n_tok, n_seq, W_B)), _full((n, W_C)), _full((n, W_C)),
                  _full((n, D_MODEL)), _full((W_B, W_B)), _full((1, W_B)), _full((1, W_C)), _full((1, W_C)),
                  _full((n_tok * n_tok, W_C)), _full((n_tok, W_C)), _full((D_MODEL, D_MODEL)),
                  _full((1, D_MODEL)), _full((D_MODEL, LANES)), _full((D_MODEL, LANES)), _full((1, LANES))],
        out_specs=[_full((n, D_MODEL)), _full((n, D_MODEL)), _full((n, LANES)), _full((n, W_C))],
        out_shape=[jax.ShapeDtypeStruct((n, D_MODEL), F32), jax.ShapeDtypeStruct((n, D_MODEL), BF16),
                   jax.ShapeDtypeStruct((n, LANES), F32), jax.ShapeDtypeStruct((n, W_C), F32)],
        compiler_params=_cparams(("arbitrary",)),
        name="mix_sample",
    )(a, pext, u, gv, x, lw["wpool"], lw["pscale"], lw["sgu_g"], lw["sgu_b"], lw["wl_new"], lw["bl_new"],
      lw["w_out"], lw["g_ffn"], lw["wr_hi"], lw["wr_lo"], lw["br"])


def _moe_body(elo_ref, ehi_ref, nact_ref, xs_ref, wrow_ref, wg_lo, wg_hi, wu_lo, wu_hi, wd_lo, wd_hi, y_ref):
    i = pl.program_id(0)

    @pl.when(i < nact_ref[0])
    def _():
        x = xs_ref[...]
        wrow = wrow_ref[...]

        def hidden(wg, wu, gate):
            hg = jnp.dot(x, wg[...], preferred_element_type=F32)
            hu = jnp.dot(x, wu[...], preferred_element_type=F32)
            return (hg / (1.0 + jnp.exp(-hg)) * hu * gate).astype(BF16)

        h_lo = hidden(wg_lo, wu_lo, wrow[:, 0:1])
        h_hi = hidden(wg_hi, wu_hi, wrow[:, 1:2])
        y_ref[...] = (jnp.dot(h_lo, wd_lo[...], preferred_element_type=F32)
                      + jnp.dot(h_hi, wd_hi[...], preferred_element_type=F32))

    @pl.when(i >= nact_ref[0])
    def _():
        y_ref[...] = jnp.zeros(y_ref.shape, y_ref.dtype)


def _moe(xs, wrow, e_lo, e_hi, n_active, wg, wu, wd, tm):
    rows = xs.shape[0]
    up_lo = pl.BlockSpec((None, D_MODEL, D_FF), lambda i, elo, ehi, na: (elo[i], 0, 0))
    up_hi = pl.BlockSpec((None, D_MODEL, D_FF), lambda i, elo, ehi, na: (ehi[i], 0, 0))
    dn_lo = pl.BlockSpec((None, D_FF, D_MODEL), lambda i, elo, ehi, na: (elo[i], 0, 0))
    dn_hi = pl.BlockSpec((None, D_FF, D_MODEL), lambda i, elo, ehi, na: (ehi[i], 0, 0))
    grid_spec = pltpu.PrefetchScalarGridSpec(
        num_scalar_prefetch=3,
        grid=(rows // tm,),
        in_specs=[pl.BlockSpec((tm, D_MODEL), lambda i, *_: (i, 0)), pl.BlockSpec((tm, LANES), lambda i, *_: (i, 0)),
                  up_lo, up_hi, up_lo, up_hi, dn_lo, dn_hi],
        out_specs=pl.BlockSpec((tm, D_MODEL), lambda i, *_: (i, 0)),
    )
    return pl.pallas_call(
        _moe_body,
        grid_spec=grid_spec,
        out_shape=jax.ShapeDtypeStruct((rows, D_MODEL), F32),
        compiler_params=_cparams(("arbitrary",)),
        name="moe",
    )(e_lo, e_hi, n_active, xs, wrow, wg, wg, wu, wu, wd, wd)


_PAIR_LO = (0, 0, 0, 1, 1, 2)
_PAIR_HI = (1, 2, 3, 2, 3, 3)


def _moe_dispatch(xn, route, wg, wu, wd, tm):
    n = xn.shape[0]
    n_tiles = -(-n // tm) + N_BUCKETS
    rows = n_tiles * tm
    bucket = route[:, 0].astype(jnp.int32)
    order = jnp.argsort(bucket, stable=True).astype(jnp.int32)
    counts = jnp.sum(bucket[:, None] == jnp.arange(N_BUCKETS, dtype=jnp.int32)[None, :], axis=0, dtype=jnp.int32)
    padded = ((counts + tm - 1) // tm) * tm
    ends = jnp.cumsum(padded)
    starts = ends - padded
    sorted_starts = jnp.cumsum(counts) - counts
    r = jnp.arange(rows, dtype=jnp.int32)
    row_bucket = jnp.minimum(jnp.searchsorted(ends, r, side="right").astype(jnp.int32), N_BUCKETS - 1)
    idx_in = r - starts[row_bucket]
    valid = (idx_in < counts[row_bucket]) & (r < ends[-1])
    src = jnp.where(valid, order[jnp.clip(sorted_starts[row_bucket] + idx_in, 0, n - 1)], 0)
    xs = jnp.take(xn, src, axis=0)
    wrow = jnp.where(valid[:, None], jnp.take(route[:, 1:3], src, axis=0), 0.0)
    wrow = jnp.pad(wrow, ((0, 0), (0, LANES - 2)))
    tile_bucket = row_bucket[::tm]
    group = tile_bucket // N_PAIRS
    pair = tile_bucket % N_PAIRS
    e_lo = group * EXP_PER_GROUP + jnp.asarray(_PAIR_LO, jnp.int32)[pair]
    e_hi = group * EXP_PER_GROUP + jnp.asarray(_PAIR_HI, jnp.int32)[pair]
    n_active = (ends[-1] // tm).reshape(1).astype(jnp.int32)
    ys = _moe(xs, wrow, e_lo, e_hi, n_active, wg, wu, wd, tm)
    rank = jnp.zeros((n,), jnp.int32).at[order].set(jnp.arange(n, dtype=jnp.int32))
    pos = starts[bucket] + rank - sorted_starts[bucket]
    return jnp.take(ys, pos, axis=0)


def _final_body(x_ref, y_ref, g_ref, o_ref):
    o_ref[...] = _rms(x_ref[...] + y_ref[...], g_ref[...], NORM_EPS)


def _final_norm(x, y_moe, g, tm):
    n = x.shape[0]
    row = pl.BlockSpec((tm, D_MODEL), lambda i: (i, 0))
    return pl.pallas_call(
        _final_body,
        grid=(n // tm,),
        in_specs=[row, row, _full((1, D_MODEL))],
        out_specs=row,
        out_shape=jax.ShapeDtypeStruct((n, D_MODEL), F32),
        compiler_params=_cparams(("parallel",)),
        name="final_norm",
    )(x, y_moe, g.reshape(1, D_MODEL))


def _layer_weights(l, n_tok, norm_ffn_g, w_out, subln_g, w_pool, pool_scale, sgu_norm_g, sgu_norm_b,
                   w_spatial, b_spatial, w_router_group, b_router_group, w_router_expert, b_router_expert):
    wpool = jnp.zeros((W_B, W_B), F32)
    for g in range(len(POOL_WINDOWS)):
        wpool = wpool.at[g * CG_B:(g + 1) * CG_B, g * CG_B:(g + 1) * CG_B].set(w_pool[l, g])
    wr = jnp.concatenate([w_router_group[l], w_router_expert[l],
                          jnp.zeros((D_MODEL, LANES - N_GROUPS - N_EXPERTS), F32)], axis=1)
    wr_hi = wr.astype(BF16)
    br = jnp.concatenate([b_router_group[l], b_router_expert[l],
                          jnp.zeros((LANES - N_GROUPS - N_EXPERTS,), F32)]).reshape(1, LANES)
    per_lane = lambda t: jnp.repeat(t, HD_C, axis=-1)
    ws_new = jnp.tril(w_spatial[l, :, :n_tok, :n_tok])
    return dict(
        wpool=wpool.astype(BF16), pscale=pool_scale[l].reshape(1, W_B),
        sgu_g=sgu_norm_g[l].reshape(1, W_C), sgu_b=sgu_norm_b[l].reshape(1, W_C),
        w_s=w_spatial[l], bs_chunk=per_lane(b_spatial[l].T),
        wl_new=per_lane(jnp.transpose(ws_new, (1, 2, 0)).reshape(n_tok * n_tok, H_C)),
        bl_new=per_lane(b_spatial[l, :, :n_tok].T),
        w_out=w_out[l].astype(BF16), g_ffn=norm_ffn_g[l].reshape(1, D_MODEL),
        wr_hi=wr_hi, wr_lo=(wr - wr_hi.astype(F32)).astype(BF16), br=br,
        sg=subln_g[l].reshape(1, 2 * HD_A),
    )


def kernel(x_prompt, x_sample, cache_k, cache_v, state_pool, page_table, norm_mix_g, norm_ffn_g, w_in, w_out, lambda_q1, lambda_k1, lambda_q2, lambda_k2, subln_g, w_pool, pool_scale, sgu_norm_g, sgu_norm_b, w_spatial, b_spatial, w_router_group, b_router_group, w_router_expert, b_router_expert, w_exp_gate, w_exp_up, w_exp_down, final_norm_g):
    batch, seq, _ = x_prompt.shape
    n_seq, n_tok, _ = x_sample.shape
    depth = w_in.shape[0]
    n_p = batch * seq
    n_s = n_seq * n_tok
    assert seq % TM_MIX == 0 and seq % TQ_ATTN == 0 and n_p % TM_INPROJ == 0

    xp = x_prompt.reshape(n_p, D_MODEL)
    xs = jnp.transpose(x_sample, (1, 0, 2)).reshape(n_s, D_MODEL)
    yp_moe = ys_moe = None
    ck = cache_k.reshape(depth, cache_k.shape[1], PAGE_SIZE * H_A, 2 * HD_A)
    cv = cache_v.reshape(depth, cache_v.shape[1], PAGE_SIZE * H_A, 2 * HD_A)

    def to_seq_major(t, width):
        return jnp.transpose(t.reshape(n_tok, n_seq, width), (1, 0, 2))

    kp_l, vp_l, pp_l, ks_l, vs_l, ps_l, cs_l = [], [], [], [], [], [], []
    for l in range(depth):
        li = _lambda_init(l)
        lw = _layer_weights(l, n_tok, norm_ffn_g, w_out, subln_g, w_pool, pool_scale, sgu_norm_g, sgu_norm_b,
                            w_spatial, b_spatial, w_router_group, b_router_group, w_router_expert, b_router_expert)
        lam_params = tuple(t[l].reshape(1, HD_A) for t in (lambda_q1, lambda_k1, lambda_q2, lambda_k2))
        w_in_bf = w_in[l].astype(BF16)
        wg, wu, wd = w_exp_gate[l].astype(BF16), w_exp_up[l].astype(BF16), w_exp_down[l].astype(BF16)

        xp, (q, k, v, kb, vb, p, u, gv) = _inproj(xp, yp_moe, norm_mix_g[l], w_in_bf, TM_INPROJ)
        a = _attn_prompt(q, kb, vb, lam_params, lw["sg"], li, batch, seq)
        xp, xn, route = _mix_prompt(a, p, u, gv, xp, lw, seq)
        yp_moe = _moe_dispatch(xn, route, wg, wu, wd, TM_MOE)
        kp_l.append(k.reshape(batch, seq, H_A, 2 * HD_A))
        vp_l.append(v.reshape(batch, seq, H_A, 2 * HD_A))
        pp_l.append(p.reshape(batch, seq, W_B)[:, seq - POOL_BUF:])

        xs, (q, k, v, _, _, p, u, gv) = _inproj(xs, ys_moe, norm_mix_g[l], w_in_bf, n_s)
        per_seq = lambda t: to_seq_major(t, W_A).reshape(n_seq, n_tok * H_A, 2 * HD_A)
        a = _attn_sample(per_seq(q), per_seq(k), per_seq(v), ck, cv, page_table, l, lam_params, lw["sg"], li, n_tok)
        a = jnp.transpose(a.reshape(n_seq, n_tok, W_A), (1, 0, 2)).reshape(n_s, W_A)
        pext = jnp.concatenate([jnp.transpose(state_pool[l], (1, 0, 2)), p.reshape(n_tok, n_seq, W_B)], axis=0)
        xs, xn, route, vn = _mix_sample(a, pext, u, gv, xs, lw, n_tok, n_seq)
        ys_moe = _moe_dispatch(xn, route, wg, wu, wd, TM_MOE_SAMPLE)
        ks_l.append(to_seq_major(k, W_A).reshape(n_seq, n_tok, H_A, 2 * HD_A))
        vs_l.append(to_seq_major(v, W_A).reshape(n_seq, n_tok, H_A, 2 * HD_A))
        ps_l.append(jnp.transpose(pext[n_tok:], (1, 0, 2)))
        cs_l.append(to_seq_major(vn, W_C))

    y_prompt = _final_norm(xp, yp_moe, final_norm_g, TM_FINAL).reshape(batch, seq, D_MODEL)
    y_sample = to_seq_major(_final_norm(xs, ys_moe, final_norm_g, n_s), D_MODEL)
    return (y_prompt, y_sample, jnp.stack(kp_l), jnp.stack(vp_l), jnp.stack(pp_l),
            jnp.stack(ks_l), jnp.stack(vs_l), jnp.stack(ps_l), jnp.stack(cs_l))
```

```python
import functools
import math

import jax
import jax.numpy as jnp
from jax import lax
from jax.experimental import pallas as pl
from jax.experimental.pallas import tpu as pltpu

F32 = jnp.float32
BF16 = jnp.bfloat16
HIGHEST = lax.Precision.HIGHEST

D_MODEL = 1024
H_A = 4
HD_A = 64
W_A = H_A * 2 * HD_A
POOL_WINDOWS = (2, 4, 8, 16)
CG_B = 64
W_B = len(POOL_WINDOWS) * CG_B
POOL_BUF = max(POOL_WINDOWS) - 1
H_C = 4
HD_C = 64
W_C = H_C * HD_C
CHUNK = 128
D_IN = 3 * W_A + W_B + 2 * W_C
N_GROUPS = 4
EXP_PER_GROUP = 4
N_EXPERTS = N_GROUPS * EXP_PER_GROUP
N_PAIRS = 6
N_BUCKETS = N_GROUPS * N_PAIRS
D_FF = D_MODEL // 2
PAGE_SIZE = 128
NORM_EPS = 1e-6
SUBLN_EPS = 1e-5

LANES = 128
SUBLANES = 8
POOL_HALO = 16
VMEM_LIMIT = 56 * 1024 * 1024
D_ROUTED = D_MODEL + LANES
ROUTE_FIELDS = SUBLANES

Q_SCALE = (HD_A ** -0.5) * math.log2(math.e)

TM_INPROJ = 512
TQ_ATTN = 512
TM_MIX = 256
TM_MOE = 256
TM_MOE_SAMPLE = 16
TM_FINAL = 512
PAGES_PER_STEP = 16


def _lambda_init(layer):
    return 0.8 - 0.6 * math.exp(-0.3 * layer)


def _cparams(sem):
    return pltpu.CompilerParams(dimension_semantics=sem, vmem_limit_bytes=VMEM_LIMIT)


def _full(shape):
    return pl.BlockSpec(shape, lambda *_: (0,) * len(shape))


_ANY = pl.BlockSpec(memory_space=pl.ANY)


def _rms(x, g, eps, axis=-1):
    return x * lax.rsqrt(jnp.mean(x * x, axis=axis, keepdims=True) + eps) * g


def _gelu(x):
    return 0.5 * x * (1.0 + lax.erf(x * (2.0 ** -0.5)))


def _inproj_prompt_body(*refs, has_moe, has_prev, tm, tq):
    refs = list(refs)
    x_ref = refs.pop(0)
    x = x_ref[...]
    if has_moe:
        x = x + refs.pop(0)[...]
    g_ref, w_ref = refs.pop(0), refs.pop(0)
    if has_prev:
        refs.pop(0), refs.pop(0)
    if has_moe:
        refs.pop(0)[...] = x
    qt_ref, k_ref, v_ref, kb_ref, vt_ref, p_ref, u_ref, gv_ref = refs
    xn = _rms(x, g_ref[...], NORM_EPS).astype(BF16)

    def seg(lo, hi):
        return jnp.dot(xn, w_ref[:, lo:hi], preferred_element_type=F32)

    def put_feature_major(ref, t):
        tt = t.T.astype(BF16)
        for h in range(H_A):
            for s in range(tm // tq):
                ref[h, s] = tt[h * LANES:(h + 1) * LANES, s * tq:(s + 1) * tq]

    def put_head_rows(ref, t):
        for h in range(H_A):
            ref[pl.ds(h, tm, stride=H_A), :] = t[:, h * LANES:(h + 1) * LANES]

    put_feature_major(qt_ref, seg(0, W_A) * Q_SCALE)
    k = seg(W_A, 2 * W_A)
    put_head_rows(k_ref, k)
    kb_ref[...] = k.astype(BF16)
    v = seg(2 * W_A, 3 * W_A)
    put_head_rows(v_ref, v)
    put_feature_major(vt_ref, v)
    o = 3 * W_A
    p_ref[...] = seg(o, o + W_B)
    u_ref[...] = seg(o + W_B, o + W_B + W_C)
    gv_ref[...] = seg(o + W_B + W_C, D_IN)


def _inproj_prompt(x, y_moe, g, w_bf, layer, depth, kv_prev):
    n = x.shape[0]
    tm, tq = TM_INPROJ, TQ_ATTN
    has_moe = y_moe is not None
    has_prev = kv_prev is not None
    row = lambda w: pl.BlockSpec((tm, w), lambda i: (i, 0))
    fmajor = pl.BlockSpec((H_A, tm // tq, LANES, tq), lambda i: (0, i, 0, 0))
    kv_rows = pl.BlockSpec((None, tm * H_A, LANES), lambda i: (layer, i, 0))
    in_specs = [row(D_MODEL)] + ([row(D_MODEL)] if has_moe else []) + [_full((1, D_MODEL)), _full((D_MODEL, D_IN))]
    args = [x] + ([y_moe] if has_moe else []) + [g.reshape(1, D_MODEL), w_bf]
    aliases = {}
    n_lead_out = 1 if has_moe else 0
    if has_prev:
        aliases = {len(args): n_lead_out + 1, len(args) + 1: n_lead_out + 2}
        in_specs += [_ANY, _ANY]
        args += list(kv_prev)
    fm_shape = jax.ShapeDtypeStruct((H_A, n // tq, LANES, tq), BF16)
    kv_shape = jax.ShapeDtypeStruct((depth, n * H_A, LANES), F32)
    outs = [
        (fm_shape, fmajor),
        (kv_shape, kv_rows),
        (kv_shape, kv_rows),
        (jax.ShapeDtypeStruct((n, W_A), BF16), row(W_A)),
        (fm_shape, fmajor),
        (jax.ShapeDtypeStruct((n, W_B), F32), row(W_B)),
        (jax.ShapeDtypeStruct((n, W_C), F32), row(W_C)),
        (jax.ShapeDtypeStruct((n, W_C), F32), row(W_C)),
    ]
    if has_moe:
        outs = [(jax.ShapeDtypeStruct((n, D_MODEL), F32), row(D_MODEL))] + outs
    res = pl.pallas_call(
        functools.partial(_inproj_prompt_body, has_moe=has_moe, has_prev=has_prev, tm=tm, tq=tq),
        grid=(n // tm,),
        in_specs=in_specs,
        out_specs=[s for _, s in outs],
        out_shape=[s for s, _ in outs],
        input_output_aliases=aliases,
        compiler_params=_cparams(("parallel",)),
        name="inproj_prompt",
    )(*args)
    if has_moe:
        return res
    return [x] + list(res)


def _inproj_sample_body(*refs, has_moe):
    refs = list(refs)
    x = refs.pop(0)[...]
    if has_moe:
        x = x + refs.pop(0)[...]
    g_ref, w_ref = refs.pop(0), refs.pop(0)
    if has_moe:
        refs.pop(0)[...] = x
    q_ref, k_ref, v_ref, p_ref, u_ref, gv_ref = refs
    xn = _rms(x, g_ref[...], NORM_EPS)

    def seg(lo, hi):
        return jnp.dot(xn, w_ref[:, lo:hi], preferred_element_type=F32, precision=HIGHEST)

    q_ref[...] = (seg(0, W_A) * Q_SCALE).astype(BF16)
    k_ref[...] = seg(W_A, 2 * W_A)
    v_ref[...] = seg(2 * W_A, 3 * W_A)
    o = 3 * W_A
    p_ref[...] = seg(o, o + W_B)
    u_ref[...] = seg(o + W_B, o + W_B + W_C)
    gv_ref[...] = seg(o + W_B + W_C, D_IN)


def _inproj_sample(x, y_moe, g, w_f32):
    n = x.shape[0]
    has_moe = y_moe is not None
    blk = lambda w: _full((n, w))
    outs = [(jax.ShapeDtypeStruct((n, W_A), BF16), blk(W_A)), (jax.ShapeDtypeStruct((n, W_A), F32), blk(W_A)),
            (jax.ShapeDtypeStruct((n, W_A), F32), blk(W_A)), (jax.ShapeDtypeStruct((n, W_B), F32), blk(W_B)),
            (jax.ShapeDtypeStruct((n, W_C), F32), blk(W_C)), (jax.ShapeDtypeStruct((n, W_C), F32), blk(W_C))]
    if has_moe:
        outs = [(jax.ShapeDtypeStruct((n, D_MODEL), F32), blk(D_MODEL))] + outs
    args = [x] + ([y_moe] if has_moe else []) + [g.reshape(1, D_MODEL), w_f32]
    res = pl.pallas_call(
        functools.partial(_inproj_sample_body, has_moe=has_moe),
        grid=(1,),
        in_specs=[blk(D_MODEL)] * (2 if has_moe else 1) + [_full((1, D_MODEL)), _full((D_MODEL, D_IN))],
        out_specs=[s for _, s in outs],
        out_shape=[s for s, _ in outs],
        compiler_params=_cparams(("arbitrary",)),
        name="inproj_sample",
    )(*args)
    if has_moe:
        return res
    return [x] + list(res)


def _diff_lambda(lq1, lk1, lq2, lk2, li):
    s1 = jnp.sum(lq1[...] * lk1[...], axis=-1, keepdims=True)
    s2 = jnp.sum(lq2[...] * lk2[...], axis=-1, keepdims=True)
    return jnp.exp(s1) - jnp.exp(s2) + li


def _split_maps(q, axis):
    idx = lax.broadcasted_iota(jnp.int32, q.shape, axis)
    zero = jnp.zeros_like(q)
    return jnp.where(idx < HD_A, q, zero), jnp.where(idx >= HD_A, q, zero)


def _attn_prompt_body(qt_ref, k_ref, vt_ref, lq1, lk1, lq2, lk2, sg_ref, o_ref, a0_sc, a1_sc, *, tq, li):
    i = pl.program_id(2)
    qz = _split_maps(qt_ref[...], 0)
    acc = (a0_sc, a1_sc)
    for a in acc:
        a[...] = jnp.zeros(a.shape, F32)

    def step(j, stats, masked):
        start = pl.multiple_of(j * tq, tq)
        kb = k_ref[pl.ds(start, tq), :]
        vt = vt_ref[j]
        ss = [jnp.dot(kb, qz[m], preferred_element_type=F32) for m in range(2)]
        if masked:
            key = lax.broadcasted_iota(jnp.int32, ss[0].shape, 0)
            qry = lax.broadcasted_iota(jnp.int32, ss[0].shape, 1)
            ss = [jnp.where(key <= qry, s, -jnp.inf) for s in ss]
        out = []
        for m in range(2):
            m_old, l_old = stats[2 * m], stats[2 * m + 1]
            m_new = jnp.maximum(m_old, jnp.max(ss[m], axis=0, keepdims=True))
            alpha = jnp.exp2(m_old - m_new)
            p = jnp.exp2(ss[m] - m_new)
            out += [m_new, alpha * l_old + jnp.sum(p, axis=0, keepdims=True)]
            acc[m][...] = alpha * acc[m][...] + jnp.dot(vt, p.astype(BF16), preferred_element_type=F32)
        return tuple(out)

    neg = jnp.full((1, tq), -jnp.inf, F32)
    zero = jnp.zeros((1, tq), F32)
    stats = lax.fori_loop(0, i, lambda j, st: step(j, st, False), (neg, zero, neg, zero))
    _, l0, _, l1 = step(i, stats, True)

    lam = _diff_lambda(lq1, lk1, lq2, lk2, li)
    o = a0_sc[...] / l0 - lam * (a1_sc[...] / l1)
    r = _rms(o, sg_ref[...], SUBLN_EPS, axis=0) * (1.0 - li)
    o_ref[...] = r.T.astype(BF16)


def _attn_prompt(qt, kb, vt, lam_params, sg_col, li, batch, seq):
    tq = TQ_ATTN
    nq = seq // tq
    q_blk = pl.BlockSpec((None, None, LANES, tq), lambda b, h, i: (h, b * nq + i, 0, 0))
    k_blk = pl.BlockSpec((seq, LANES), lambda b, h, i: (b, h))
    v_blk = pl.BlockSpec((None, nq, LANES, tq), lambda b, h, i: (h, b, 0, 0))
    return pl.pallas_call(
        functools.partial(_attn_prompt_body, tq=tq, li=li),
        grid=(batch, H_A, nq),
        in_specs=[q_blk, k_blk, v_blk] + [_full((1, HD_A))] * 4 + [_full((2 * HD_A, 1))],
        out_specs=pl.BlockSpec((tq, LANES), lambda b, h, i: (b * nq + i, h)),
        out_shape=jax.ShapeDtypeStruct((batch * seq, W_A), BF16),
        scratch_shapes=[pltpu.VMEM((LANES, tq), F32), pltpu.VMEM((LANES, tq), F32)],
        compiler_params=_cparams(("parallel", "parallel", "arbitrary")),
        name="attn_prompt",
    )(qt, kb, vt, *lam_params, sg_col)


def _attn_sample_body(pt_ref, q_ref, kn_ref, vn_ref, *rest, n_pages_step, n_tok, li):
    kp = rest[:n_pages_step]
    vp = rest[n_pages_step:2 * n_pages_step]
    lq1, lk1, lq2, lk2, sg_ref, o_ref, m_sc, l_sc, a_sc = rest[2 * n_pages_step:]
    c = pl.program_id(1)
    nq = n_tok * H_A

    @pl.when(c == 0)
    def _():
        m_sc[...] = jnp.full(m_sc.shape, -jnp.inf, F32)
        l_sc[...] = jnp.zeros(l_sc.shape, F32)
        a_sc[...] = jnp.zeros(a_sc.shape, F32)

    q1, q2 = _split_maps(q_ref[...], 1)
    qb = jnp.concatenate([q1, q2], axis=0)

    def accumulate(blocks):
        ss = []
        for k, _, mask in blocks:
            s = lax.dot_general(qb, k.astype(BF16), (((1,), (1,)), ((), ())), preferred_element_type=F32)
            ss.append(jnp.where(mask, s, -jnp.inf))
        m_old = m_sc[...]
        m_new = m_old
        for s in ss:
            m_new = jnp.maximum(m_new, jnp.max(s, axis=-1, keepdims=True))
        alpha = jnp.exp2(m_old - m_new)
        l_new = alpha * l_sc[...]
        a_new = alpha * a_sc[...]
        for s, (_, v, _) in zip(ss, blocks):
            p = jnp.exp2(s - m_new)
            l_new = l_new + jnp.sum(p, axis=-1, keepdims=True)
            a_new = a_new + jnp.dot(p.astype(BF16), v.astype(BF16), preferred_element_type=F32)
        m_sc[...] = m_new
        l_sc[...] = l_new
        a_sc[...] = a_new

    n_rows = PAGE_SIZE * H_A
    r = lax.broadcasted_iota(jnp.int32, (2 * nq, n_rows), 0)
    col = lax.broadcasted_iota(jnp.int32, (2 * nq, n_rows), 1)
    head_of = lambda t: t & (H_A - 1)
    head_mask = head_of(r) == head_of(col)
    accumulate([(kp[j][...], vp[j][...], head_mask) for j in range(n_pages_step)])

    @pl.when(c == pl.num_programs(1) - 1)
    def _():
        rn = lax.broadcasted_iota(jnp.int32, (2 * nq, nq), 0)
        cn = lax.broadcasted_iota(jnp.int32, (2 * nq, nq), 1)
        q_row = jnp.where(rn >= nq, rn - nq, rn)
        new_mask = (head_of(rn) == head_of(cn)) & (cn <= q_row)
        accumulate([(kn_ref[...], vn_ref[...], new_mask)])
        lam = _diff_lambda(lq1, lk1, lq2, lk2, li)
        o = a_sc[...] / l_sc[...]
        o_ref[...] = _rms(o[:nq] - lam * o[nq:], sg_ref[...], SUBLN_EPS) * (1.0 - li)


def _attn_sample(q, k_new, v_new, cache_k, cache_v, page_table, layer, lam_params, sg, li, n_tok):
    n_seq, n_pages = page_table.shape
    pps = math.gcd(PAGES_PER_STEP, n_pages)
    nq = n_tok * H_A
    n_rows = PAGE_SIZE * H_A

    def page_spec(j):
        return pl.BlockSpec((None, None, n_rows, LANES), lambda s, c, pt: (layer, pt[s, c * pps + j], 0, 0))

    seq_blk = pl.BlockSpec((None, nq, LANES), lambda s, c, pt: (s, 0, 0))
    const = lambda shape: pl.BlockSpec(shape, lambda s, c, pt: (0,) * len(shape))
    grid_spec = pltpu.PrefetchScalarGridSpec(
        num_scalar_prefetch=1,
        grid=(n_seq, n_pages // pps),
        in_specs=[seq_blk, seq_blk, seq_blk] + [page_spec(j) for j in range(pps)] * 2
        + [const((1, HD_A))] * 4 + [const((1, 2 * HD_A))],
        out_specs=seq_blk,
        scratch_shapes=[pltpu.VMEM((2 * nq, 1), F32), pltpu.VMEM((2 * nq, 1), F32), pltpu.VMEM((2 * nq, LANES), F32)],
    )
    return pl.pallas_call(
        functools.partial(_attn_sample_body, n_pages_step=pps, n_tok=n_tok, li=li),
        grid_spec=grid_spec,
        out_shape=jax.ShapeDtypeStruct((n_seq, nq, LANES), F32),
        compiler_params=_cparams(("parallel", "arbitrary")),
        name="attn_sample",
    )(page_table, q, k_new, v_new, *([cache_k] * pps), *([cache_v] * pps), *lam_params, sg)


def _group_select(vals):
    lane = lax.broadcasted_iota(jnp.int32, vals[0].shape, 1)
    out = vals[-1]
    for g in range(len(vals) - 2, -1, -1):
        out = jnp.where(lane < (g + 1) * CG_B, vals[g], out)
    return out


def _router(xn, wr_hi, wr_lo, br):
    hi = xn.astype(BF16)
    lo = (xn - hi.astype(F32)).astype(BF16)
    logits = (jnp.dot(hi, wr_hi[...], preferred_element_type=F32)
              + jnp.dot(lo, wr_hi[...], preferred_element_type=F32)
              + jnp.dot(hi, wr_lo[...], preferred_element_type=F32)) + br[...]
    lane = lax.broadcasted_iota(jnp.int32, logits.shape, 1).astype(F32)
    neg = -jnp.inf
    big = float(LANES)

    def first_argmax(vals):
        top = jnp.max(vals, axis=-1, keepdims=True)
        idx = jnp.min(jnp.where(vals == top, lane, big), axis=-1, keepdims=True)
        return top, idx

    is_group = lane < N_GROUPS
    g_top, g_idx = first_argmax(jnp.where(is_group, logits, neg))
    g_w = 1.0 / jnp.sum(jnp.where(is_group, jnp.exp(logits - g_top), 0.0), axis=-1, keepdims=True)
    e_base = N_GROUPS + EXP_PER_GROUP * g_idx
    in_group = (lane >= e_base) & (lane < e_base + EXP_PER_GROUP)
    e_vals = jnp.where(in_group, logits, neg)
    e1, i1 = first_argmax(e_vals)
    e2, i2 = first_argmax(jnp.where(lane == i1, neg, e_vals))
    t = jnp.exp(e2 - e1)
    w1 = g_w / (1.0 + t)
    w2 = g_w * t / (1.0 + t)
    a1 = i1 - e_base
    a2 = i2 - e_base
    lo_id = jnp.minimum(a1, a2)
    hi_id = jnp.maximum(a1, a2)
    w_lo = jnp.where(a1 < a2, w1, w2)
    w_hi = jnp.where(a1 < a2, w2, w1)
    pair_base = jnp.where(lo_id == 0.0, 0.0, jnp.where(lo_id == 1.0, 3.0, 5.0))
    bucket = g_idx * N_PAIRS + pair_base + (hi_id - lo_id - 1.0)
    return jnp.where(lane == 0.0, bucket, jnp.where(lane == 1.0, w_lo, jnp.where(lane == 2.0, w_hi, 0.0)))


def _mix_tail(a, b_out, c_out, x, wout_ref, gffn_ref, wr_hi, wr_lo, br, y_ref, xg_ref, rt_ref, cnt_ref, *, exact):
    if exact:
        mm = lambda t, lo, hi: jnp.dot(t, wout_ref[lo:hi, :], preferred_element_type=F32, precision=HIGHEST)
    else:
        mm = lambda t, lo, hi: jnp.dot(t.astype(BF16), wout_ref[lo:hi, :], preferred_element_type=F32)
    y = x + (mm(a, 0, W_A) + mm(b_out, W_A, W_A + W_B) + mm(c_out, W_A + W_B, D_MODEL))
    y_ref[...] = y
    xn = _rms(y, gffn_ref[...], NORM_EPS)
    route = _router(xn, wr_hi, wr_lo, br)
    xg_ref[:, 0:D_MODEL] = xn
    xg_ref[:, D_MODEL:] = route
    rt_ref[...] = route.T[0:ROUTE_FIELDS, :]
    lane = lax.broadcasted_iota(jnp.int32, route.shape, 1).astype(F32)
    hits = jnp.sum(jnp.where(lane == route[:, 0:1], 1.0, 0.0), axis=0, keepdims=True)

    @pl.when(pl.program_id(0) == 0)
    def _():
        cnt_ref[...] = jnp.zeros(cnt_ref.shape, F32)

    cnt_ref[...] += hits


def _sgu_norm(gv, sgu_g, sgu_b):
    gg = _gelu(gv)
    mu = jnp.mean(gg, axis=-1, keepdims=True)
    d = gg - mu
    var = jnp.mean(d * d, axis=-1, keepdims=True)
    return d * lax.rsqrt(var + NORM_EPS) * sgu_g + sgu_b


def _mix_prompt_body(a_ref, p_ref, halo_ref, u_ref, gv_ref, x_ref, wpool_ref, pscale_ref, sgug_ref, sgub_ref,
                     ws_ref, bs_ref, wout_ref, gffn_ref, wr_hi, wr_lo, br,
                     y_ref, xg_ref, rt_ref, cnt_ref, ext_sc, *, tm, tiles_per_seq):
    i = pl.program_id(0)
    t_in_seq = i % tiles_per_seq
    p = p_ref[...]
    ext_sc[0:POOL_HALO, :] = jnp.where(t_in_seq == 0, 0.0, halo_ref[...])
    ext_sc[POOL_HALO:, :] = p
    pos = t_in_seq * tm + lax.broadcasted_iota(jnp.int32, (tm, 1), 0)
    acc = p
    means = []
    for k in range(1, max(POOL_WINDOWS)):
        acc = acc + ext_sc[POOL_HALO - k:POOL_HALO - k + tm, :]
        if k + 1 in POOL_WINDOWS:
            cnt = jnp.minimum(pos + 1, k + 1).astype(F32)
            means.append(acc / cnt)
    d = (_group_select(means) - p).astype(BF16)
    b_out = jnp.dot(d, wpool_ref[...], preferred_element_type=F32) * pscale_ref[...]

    vn = _sgu_norm(gv_ref[...], sgug_ref[...], sgub_ref[...]).astype(BF16)
    r = lax.broadcasted_iota(jnp.int32, (CHUNK, CHUNK), 0)
    c = lax.broadcasted_iota(jnp.int32, (CHUNK, CHUNK), 1)
    w_tril = [jnp.where(r >= c, ws_ref[h], 0.0).astype(BF16) for h in range(H_C)]
    chunks = []
    for ci in range(tm // CHUNK):
        vc = vn[ci * CHUNK:(ci + 1) * CHUNK, :]
        per_head = [jnp.dot(w_tril[h], vc, preferred_element_type=F32) for h in range(H_C)]
        chunks.append(_group_select(per_head) + bs_ref[...])
    mixed = jnp.concatenate(chunks, axis=0)
    c_out = _gelu(u_ref[...]) * mixed

    _mix_tail(a_ref[...], b_out, c_out, x_ref[...], wout_ref, gffn_ref, wr_hi, wr_lo, br,
              y_ref, xg_ref, rt_ref, cnt_ref, exact=False)


def _mix_out_specs(n, tm):
    row = lambda w: pl.BlockSpec((tm, w), lambda i: (i, 0))
    specs = [row(D_MODEL), row(D_ROUTED), pl.BlockSpec((ROUTE_FIELDS, tm), lambda i: (0, i)), _full((1, LANES))]
    shapes = [jax.ShapeDtypeStruct((n, D_MODEL), F32), jax.ShapeDtypeStruct((n, D_ROUTED), F32),
              jax.ShapeDtypeStruct((ROUTE_FIELDS, n), F32), jax.ShapeDtypeStruct((1, LANES), F32)]
    return specs, shapes


def _mix_prompt(a, p, u, gv, x, lw, seq):
    n = x.shape[0]
    tm = TM_MIX
    tiles_per_seq = seq // tm
    halo_per_tile = tm // POOL_HALO
    row = lambda w: pl.BlockSpec((tm, w), lambda i: (i, 0))
    halo = pl.BlockSpec((POOL_HALO, W_B), lambda i: (jnp.maximum(i * halo_per_tile - 1, 0), 0))
    out_specs, out_shape = _mix_out_specs(n, tm)
    return pl.pallas_call(
        functools.partial(_mix_prompt_body, tm=tm, tiles_per_seq=tiles_per_seq),
        grid=(n // tm,),
        in_specs=[row(W_A), row(W_B), halo, row(W_C), row(W_C), row(D_MODEL),
                  _full((W_B, W_B)), _full((1, W_B)), _full((1, W_C)), _full((1, W_C)),
                  _full((H_C, CHUNK, CHUNK)), _full((CHUNK, W_C)), _full((D_MODEL, D_MODEL)),
                  _full((1, D_MODEL)), _full((D_MODEL, LANES)), _full((D_MODEL, LANES)), _full((1, LANES))],
        out_specs=out_specs,
        out_shape=out_shape,
        scratch_shapes=[pltpu.VMEM((POOL_HALO + tm, W_B), F32)],
        compiler_params=_cparams(("arbitrary",)),
        name="mix_prompt",
    )(a, p, p, u, gv, x, lw["wpool"], lw["pscale"], lw["sgu_g"], lw["sgu_b"], lw["w_s"], lw["bs_chunk"],
      lw["w_out"], lw["g_ffn"], lw["wr_hi"], lw["wr_lo"], lw["br"])


def _mix_sample_body(a_ref, pext_ref, u_ref, gv_ref, x_ref, wpool_ref, pscale_ref, sgug_ref, sgub_ref,
                     wl_ref, bl_ref, wout_ref, gffn_ref, wr_hi, wr_lo, br,
                     y_ref, xg_ref, rt_ref, cnt_ref, vn_ref, *, n_tok, n_seq):
    ds = []
    for i in range(n_tok):
        cur = pext_ref[POOL_BUF + i]
        acc = cur
        means = []
        for k in range(1, max(POOL_WINDOWS)):
            acc = acc + pext_ref[POOL_BUF + i - k]
            if k + 1 in POOL_WINDOWS:
                means.append(acc / float(k + 1))
        ds.append(_group_select(means) - cur)
    d = jnp.concatenate(ds, axis=0)
    b_out = jnp.dot(d, wpool_ref[...], preferred_element_type=F32, precision=HIGHEST) * pscale_ref[...]

    vn = _sgu_norm(gv_ref[...], sgug_ref[...], sgub_ref[...])
    vn_ref[...] = vn
    rows = []
    for i in range(n_tok):
        acc = jnp.broadcast_to(bl_ref[i:i + 1, :], (n_seq, W_C))
        for j in range(i + 1):
            acc = acc + wl_ref[i * n_tok + j:i * n_tok + j + 1, :] * vn[j * n_seq:(j + 1) * n_seq, :]
        rows.append(acc)
    c_out = _gelu(u_ref[...]) * jnp.concatenate(rows, axis=0)

    _mix_tail(a_ref[...], b_out, c_out, x_ref[...], wout_ref, gffn_ref, wr_hi, wr_lo, br,
              y_ref, xg_ref, rt_ref, cnt_ref, exact=True)


def _mix_sample(a, pext, u, gv, x, lw, n_tok, n_seq):
    n = x.shape[0]
    out_specs, out_shape = _mix_out_specs(n, n)
    return pl.pallas_call(
        functools.partial(_mix_sample_body, n_tok=n_tok, n_seq=n_seq),
        grid=(1,),
        in_specs=[_full((n, W_A)), _full((POOL_BUF + n_tok, n_seq, W_B)), _full((n, W_C)), _full((n, W_C)),
                  _full((n, D_MODEL)), _full((W_B, W_B)), _full((1, W_B)), _full((1, W_C)), _full((1, W_C)),
                  _full((n_tok * n_tok, W_C)), _full((n_tok, W_C)), _full((D_MODEL, D_MODEL)),
                  _full((1, D_MODEL)), _full((D_MODEL, LANES)), _full((D_MODEL, LANES)), _full((1, LANES))],
        out_specs=out_specs + [_full((n, W_C))],
        out_shape=out_shape + [jax.ShapeDtypeStruct((n, W_C), F32)],
        compiler_params=_cparams(("arbitrary",)),
        name="mix_sample",
    )(a, pext, u, gv, x, lw["wpool_f32"], lw["pscale"], lw["sgu_g"], lw["sgu_b"], lw["wl_new"], lw["bl_new"],
      lw["w_out_f32"], lw["g_ffn"], lw["wr_hi"], lw["wr_lo"], lw["br"])


def _moe_body(src_ref, elo_ref, ehi_ref, nv_ref, nact_ref, x_hbm, wg_lo, wg_hi, wu_lo, wu_hi, wd_lo, wd_hi,
              y_hbm, xbuf, ybuf, gsem, ssem, *, tm):
    i = pl.program_id(0)
    n_act = nact_ref[0]
    slot = lax.rem(i, 2)

    def row_copy_in(tile, s, r):
        return pltpu.make_async_copy(x_hbm.at[pl.ds(src_ref[tile * tm + r], 1)], xbuf.at[s, pl.ds(r, 1)], gsem.at[s])

    def row_copy_out(tile, s, r):
        return pltpu.make_async_copy(ybuf.at[s, pl.ds(r, 1)], y_hbm.at[pl.ds(src_ref[tile * tm + r], 1)], ssem.at[s])

    def start_gather(tile, s):
        def one(r, c):
            row_copy_in(tile, s, r).start()
            return c
        lax.fori_loop(0, tm, one, 0, unroll=8)

    def wait_gather(s):
        pltpu.make_async_copy(x_hbm.at[pl.ds(0, tm)], xbuf.at[s], gsem.at[s]).wait()

    def whole_groups(n):
        return pl.multiple_of(lax.shift_left(lax.shift_right_logical(n, 3), 3), SUBLANES)

    def start_scatter(tile, s):
        n = nv_ref[tile]

        def group(g, c):
            for u in range(SUBLANES):
                row_copy_out(tile, s, g * SUBLANES + u).start()
            return c
        lax.fori_loop(0, lax.shift_right_logical(n, 3), group, 0)

        def one(r, c):
            row_copy_out(tile, s, r).start()
            return c
        lax.fori_loop(whole_groups(n), n, one, 0)

    def wait_scatter(tile, s):
        n = nv_ref[tile]
        n_blk = whole_groups(n)

        @pl.when(n_blk > 0)
        def _():
            pltpu.make_async_copy(ybuf.at[s, pl.ds(0, n_blk)], y_hbm.at[pl.ds(0, n_blk)], ssem.at[s]).wait()

        def one(r, c):
            pltpu.make_async_copy(ybuf.at[s, pl.ds(0, 1)], y_hbm.at[pl.ds(0, 1)], ssem.at[s]).wait()
            return c
        lax.fori_loop(n_blk, n, one, 0)

    @pl.when(i == 0)
    def _():
        start_gather(0, 0)

    @pl.when(i + 1 < n_act)
    def _():
        start_gather(i + 1, 1 - slot)

    @pl.when(i < n_act)
    def _():
        wait_gather(slot)

        @pl.when(i >= 2)
        def _():
            wait_scatter(i - 2, slot)

        xrow = xbuf[slot]
        x = xrow[:, 0:D_MODEL].astype(BF16)
        gate_lo = xrow[:, D_MODEL + 1:D_MODEL + 2]
        gate_hi = xrow[:, D_MODEL + 2:D_MODEL + 3]

        def hidden(wg, wu, gate):
            hg = jnp.dot(x, wg[...], preferred_element_type=F32)
            hu = jnp.dot(x, wu[...], preferred_element_type=F32)
            return (hg / (1.0 + jnp.exp(-hg)) * hu * gate).astype(BF16)

        h_lo = hidden(wg_lo, wu_lo, gate_lo)
        h_hi = hidden(wg_hi, wu_hi, gate_hi)
        ybuf[slot] = (jnp.dot(h_lo, wd_lo[...], preferred_element_type=F32)
                      + jnp.dot(h_hi, wd_hi[...], preferred_element_type=F32))
        start_scatter(i, slot)

        @pl.when(i == n_act - 1)
        def _():
            wait_scatter(i, slot)

            @pl.when(i >= 1)
            def _():
                wait_scatter(i - 1, 1 - slot)


def _moe(xg, src, e_lo, e_hi, n_valid, n_active, wg, wu, wd, tm):
    n = xg.shape[0]
    n_tiles = e_lo.shape[0]
    wspec = lambda shape, which: pl.BlockSpec(
        (None,) + shape, lambda i, src, elo, ehi, nv, na: ((elo, ehi)[which][i], 0, 0))
    up, dn = (D_MODEL, D_FF), (D_FF, D_MODEL)
    grid_spec = pltpu.PrefetchScalarGridSpec(
        num_scalar_prefetch=5,
        grid=(n_tiles,),
        in_specs=[_ANY, wspec(up, 0), wspec(up, 1), wspec(up, 0), wspec(up, 1), wspec(dn, 0), wspec(dn, 1)],
        out_specs=_ANY,
        scratch_shapes=[pltpu.VMEM((2, tm, D_ROUTED), F32), pltpu.VMEM((2, tm, D_MODEL), F32),
                        pltpu.SemaphoreType.DMA((2,)), pltpu.SemaphoreType.DMA((2,))],
    )
    return pl.pallas_call(
        functools.partial(_moe_body, tm=tm),
        grid_spec=grid_spec,
        out_shape=jax.ShapeDtypeStruct((n, D_MODEL), F32),
        compiler_params=_cparams(("arbitrary",)),
        name="moe",
    )(src, e_lo, e_hi, n_valid, n_active, xg, wg, wg, wu, wu, wd, wd)


_PAIR_LO = (0, 0, 0, 1, 1, 2)
_PAIR_HI = (1, 2, 3, 2, 3, 3)


def _moe_dispatch(xg, route_t, cnt, wg, wu, wd, tm):
    n = xg.shape[0]
    n_tiles = -(-n // tm) + N_BUCKETS
    bucket = route_t[0].astype(jnp.int32)
    order = jnp.argsort(bucket, stable=True).astype(jnp.int32)
    counts = cnt[0, :N_BUCKETS].astype(jnp.int32)
    padded = ((counts + tm - 1) // tm) * tm
    ends = jnp.cumsum(padded)
    starts = ends - padded
    sorted_starts = jnp.cumsum(counts) - counts
    tile_start = jnp.arange(n_tiles, dtype=jnp.int32) * tm
    tile_bucket = jnp.minimum(jnp.sum(tile_start[:, None] >= ends[None, :], axis=1, dtype=jnp.int32), N_BUCKETS - 1)
    in_bucket = tile_start - starts[tile_bucket]
    n_valid = jnp.where(tile_start < ends[-1], jnp.clip(counts[tile_bucket] - in_bucket, 0, tm), 0).astype(jnp.int32)
    r_in = jnp.arange(tm, dtype=jnp.int32)[None, :]
    sorted_idx = (sorted_starts[tile_bucket] + in_bucket)[:, None] + r_in
    src = jnp.where(r_in < n_valid[:, None], order[jnp.clip(sorted_idx, 0, n - 1)], 0).reshape(-1)
    group = tile_bucket // N_PAIRS
    pair = tile_bucket % N_PAIRS
    e_lo = group * EXP_PER_GROUP + jnp.asarray(_PAIR_LO, jnp.int32)[pair]
    e_hi = group * EXP_PER_GROUP + jnp.asarray(_PAIR_HI, jnp.int32)[pair]
    n_active = (ends[-1] // tm).reshape(1).astype(jnp.int32)
    return _moe(xg, src.astype(jnp.int32), e_lo, e_hi, n_valid, n_active, wg, wu, wd, tm)


def _final_body(x_ref, y_ref, g_ref, o_ref):
    o_ref[...] = _rms(x_ref[...] + y_ref[...], g_ref[...], NORM_EPS)


def _final_norm(x, y_moe, g, tm):
    n = x.shape[0]
    row = pl.BlockSpec((tm, D_MODEL), lambda i: (i, 0))
    return pl.pallas_call(
        _final_body,
        grid=(n // tm,),
        in_specs=[row, row, _full((1, D_MODEL))],
        out_specs=row,
        out_shape=jax.ShapeDtypeStruct((n, D_MODEL), F32),
        compiler_params=_cparams(("parallel",)),
        name="final_norm",
    )(x, y_moe, g.reshape(1, D_MODEL))


def _layer_weights(l, n_tok, norm_ffn_g, w_out, subln_g, w_pool, pool_scale, sgu_norm_g, sgu_norm_b,
                   w_spatial, b_spatial, w_router_group, b_router_group, w_router_expert, b_router_expert):
    wpool = jnp.zeros((W_B, W_B), F32)
    for g in range(len(POOL_WINDOWS)):
        wpool = wpool.at[g * CG_B:(g + 1) * CG_B, g * CG_B:(g + 1) * CG_B].set(w_pool[l, g])
    wr = jnp.concatenate([w_router_group[l], w_router_expert[l],
                          jnp.zeros((D_MODEL, LANES - N_GROUPS - N_EXPERTS), F32)], axis=1)
    wr_hi = wr.astype(BF16)
    br = jnp.concatenate([b_router_group[l], b_router_expert[l],
                          jnp.zeros((LANES - N_GROUPS - N_EXPERTS,), F32)]).reshape(1, LANES)
    per_lane = lambda t: jnp.repeat(t, HD_C, axis=-1)
    ws_new = jnp.tril(w_spatial[l, :, :n_tok, :n_tok])
    return dict(
        wpool=wpool.astype(BF16), wpool_f32=wpool, pscale=pool_scale[l].reshape(1, W_B),
        sgu_g=sgu_norm_g[l].reshape(1, W_C), sgu_b=sgu_norm_b[l].reshape(1, W_C),
        w_s=w_spatial[l], bs_chunk=per_lane(b_spatial[l].T),
        wl_new=per_lane(jnp.transpose(ws_new, (1, 2, 0)).reshape(n_tok * n_tok, H_C)),
        bl_new=per_lane(b_spatial[l, :, :n_tok].T),
        w_out=w_out[l].astype(BF16), w_out_f32=w_out[l], g_ffn=norm_ffn_g[l].reshape(1, D_MODEL),
        wr_hi=wr_hi, wr_lo=(wr - wr_hi.astype(F32)).astype(BF16), br=br,
        sg=subln_g[l].reshape(1, 2 * HD_A), sg_col=subln_g[l].reshape(2 * HD_A, 1),
    )


def kernel(x_prompt, x_sample, cache_k, cache_v, state_pool, page_table, norm_mix_g, norm_ffn_g, w_in, w_out, lambda_q1, lambda_k1, lambda_q2, lambda_k2, subln_g, w_pool, pool_scale, sgu_norm_g, sgu_norm_b, w_spatial, b_spatial, w_router_group, b_router_group, w_router_expert, b_router_expert, w_exp_gate, w_exp_up, w_exp_down, final_norm_g):
    batch, seq, _ = x_prompt.shape
    n_seq, n_tok, _ = x_sample.shape
    depth = w_in.shape[0]
    n_p = batch * seq
    n_s = n_seq * n_tok
    assert seq % TM_MIX == 0 and seq % TQ_ATTN == 0 and n_p % TM_INPROJ == 0 and TM_INPROJ % TQ_ATTN == 0
    assert n_s % SUBLANES == 0 and page_table.shape[1] * PAGE_SIZE >= max(POOL_WINDOWS)

    xp = x_prompt.reshape(n_p, D_MODEL)
    xs = jnp.transpose(x_sample, (1, 0, 2)).reshape(n_s, D_MODEL)
    yp_moe = ys_moe = None
    kv_prompt = tuple(jnp.zeros((depth, n_p * H_A, 2 * HD_A), F32) for _ in range(2))
    ck = cache_k.reshape(depth, cache_k.shape[1], PAGE_SIZE * H_A, 2 * HD_A)
    cv = cache_v.reshape(depth, cache_v.shape[1], PAGE_SIZE * H_A, 2 * HD_A)

    def to_seq_major(t, width):
        return jnp.transpose(t.reshape(n_tok, n_seq, width), (1, 0, 2))

    pp_l, ks_l, vs_l, ps_l, cs_l = [], [], [], [], []
    for l in range(depth):
        li = _lambda_init(l)
        lw = _layer_weights(l, n_tok, norm_ffn_g, w_out, subln_g, w_pool, pool_scale, sgu_norm_g, sgu_norm_b,
                            w_spatial, b_spatial, w_router_group, b_router_group, w_router_expert, b_router_expert)
        lam_params = tuple(t[l].reshape(1, HD_A) for t in (lambda_q1, lambda_k1, lambda_q2, lambda_k2))
        wg, wu, wd = w_exp_gate[l].astype(BF16), w_exp_up[l].astype(BF16), w_exp_down[l].astype(BF16)

        xp, qt, k_all, v_all, kb, vt, p, u, gv = _inproj_prompt(
            xp, yp_moe, norm_mix_g[l], w_in[l].astype(BF16), l, depth, kv_prompt)
        kv_prompt = (k_all, v_all)
        a = _attn_prompt(qt, kb, vt, lam_params, lw["sg_col"], li, batch, seq)
        xp, xg, route_t, cnt = _mix_prompt(a, p, u, gv, xp, lw, seq)
        yp_moe = _moe_dispatch(xg, route_t, cnt, wg, wu, wd, TM_MOE)
        pp_l.append(p.reshape(batch, seq, W_B)[:, seq - POOL_BUF:])

        xs, q, k, v, p, u, gv = _inproj_sample(xs, ys_moe, norm_mix_g[l], w_in[l])
        per_seq = lambda t: to_seq_major(t, W_A).reshape(n_seq, n_tok * H_A, 2 * HD_A)
        a = _attn_sample(per_seq(q), per_seq(k), per_seq(v), ck, cv, page_table, l, lam_params, lw["sg"], li, n_tok)
        a = jnp.transpose(a.reshape(n_seq, n_tok, W_A), (1, 0, 2)).reshape(n_s, W_A)
        pext = jnp.concatenate([jnp.transpose(state_pool[l], (1, 0, 2)), p.reshape(n_tok, n_seq, W_B)], axis=0)
        xs, xg, route_t, cnt, vn = _mix_sample(a, pext, u, gv, xs, lw, n_tok, n_seq)
        ys_moe = _moe_dispatch(xg, route_t, cnt, wg, wu, wd, TM_MOE_SAMPLE)
        ks_l.append(to_seq_major(k, W_A).reshape(n_seq, n_tok, H_A, 2 * HD_A))
        vs_l.append(to_seq_major(v, W_A).reshape(n_seq, n_tok, H_A, 2 * HD_A))
        ps_l.append(jnp.transpose(pext[n_tok:], (1, 0, 2)))
        cs_l.append(to_seq_major(vn, W_C))

    y_prompt = _final_norm(xp, yp_moe, final_norm_g, TM_FINAL).reshape(batch, seq, D_MODEL)
    y_sample = to_seq_major(_final_norm(xs, ys_moe, final_norm_g, n_s), D_MODEL)
    k_prompt = kv_prompt[0].reshape(depth, batch, seq, H_A, 2 * HD_A)
    v_prompt = kv_prompt[1].reshape(depth, batch, seq, H_A, 2 * HD_A)
    return (y_prompt, y_sample, k_prompt, v_prompt, jnp.stack(pp_l),
            jnp.stack(ks_l), jnp.stack(vs_l), jnp.stack(ps_l), jnp.stack(cs_l))
```

```python
import functools
import math

import jax
import jax.numpy as jnp
from jax import lax
from jax.experimental import pallas as pl
from jax.experimental.pallas import tpu as pltpu

F32 = jnp.float32
BF16 = jnp.bfloat16
HIGHEST = lax.Precision.HIGHEST

D_MODEL = 1024
H_A = 4
HD_A = 64
W_A = H_A * 2 * HD_A
POOL_WINDOWS = (2, 4, 8, 16)
CG_B = 64
W_B = len(POOL_WINDOWS) * CG_B
POOL_BUF = max(POOL_WINDOWS) - 1
H_C = 4
HD_C = 64
W_C = H_C * HD_C
CHUNK = 128
D_IN = 3 * W_A + W_B + 2 * W_C
N_GROUPS = 4
EXP_PER_GROUP = 4
N_EXPERTS = N_GROUPS * EXP_PER_GROUP
N_PAIRS = 6
N_BUCKETS = N_GROUPS * N_PAIRS
D_FF = D_MODEL // 2
PAGE_SIZE = 128
NORM_EPS = 1e-6
SUBLN_EPS = 1e-5

LANES = 128
SUBLANES = 8
POOL_HALO = 16
VMEM_LIMIT = 56 * 1024 * 1024
D_ROUTED = D_MODEL + LANES
ROUTE_FIELDS = SUBLANES

Q_SCALE = (HD_A ** -0.5) * math.log2(math.e)

TM_INPROJ = 512
TQ_ATTN = 512
TM_MIX = 256
TM_MOE = 256
TM_MOE_SAMPLE = 16
TM_FINAL = 512
PAGES_PER_STEP = 16
PAGE_BUFFERS = 3


def _lambda_init(layer):
    return 0.8 - 0.6 * math.exp(-0.3 * layer)


def _cparams(sem):
    return pltpu.CompilerParams(dimension_semantics=sem, vmem_limit_bytes=VMEM_LIMIT)


def _full(shape):
    return pl.BlockSpec(shape, lambda *_: (0,) * len(shape))


_ANY = pl.BlockSpec(memory_space=pl.ANY)


def _rms(x, g, eps, axis=-1):
    return x * lax.rsqrt(jnp.mean(x * x, axis=axis, keepdims=True) + eps) * g


def _gelu(x):
    return 0.5 * x * (1.0 + lax.erf(x * (2.0 ** -0.5)))


def _inproj_prompt_body(*refs, has_moe, has_prev, tm, tq):
    refs = list(refs)
    x_ref = refs.pop(0)
    x = x_ref[...]
    if has_moe:
        x = x + refs.pop(0)[...]
    g_ref, w_ref = refs.pop(0), refs.pop(0)
    if has_prev:
        refs.pop(0), refs.pop(0)
    if has_moe:
        refs.pop(0)[...] = x
    qt_ref, k_ref, v_ref, kb_ref, vt_ref, p_ref, u_ref, gv_ref = refs
    xn = _rms(x, g_ref[...], NORM_EPS).astype(BF16)

    def seg(lo, hi):
        return jnp.dot(xn, w_ref[:, lo:hi], preferred_element_type=F32)

    def put_feature_major(ref, t):
        tt = t.T.astype(BF16)
        for h in range(H_A):
            for s in range(tm // tq):
                ref[h, s] = tt[h * LANES:(h + 1) * LANES, s * tq:(s + 1) * tq]

    def put_head_rows(ref, t):
        for h in range(H_A):
            ref[pl.ds(h, tm, stride=H_A), :] = t[:, h * LANES:(h + 1) * LANES]

    put_feature_major(qt_ref, seg(0, W_A) * Q_SCALE)
    k = seg(W_A, 2 * W_A)
    put_head_rows(k_ref, k)
    kb_ref[...] = k.astype(BF16)
    v = seg(2 * W_A, 3 * W_A)
    put_head_rows(v_ref, v)
    put_feature_major(vt_ref, v)
    o = 3 * W_A
    p_ref[...] = seg(o, o + W_B)
    u_ref[...] = seg(o + W_B, o + W_B + W_C)
    gv_ref[...] = seg(o + W_B + W_C, D_IN)


def _inproj_prompt(x, y_moe, g, w_bf, layer, depth, kv_prev):
    n = x.shape[0]
    tm, tq = TM_INPROJ, TQ_ATTN
    has_moe = y_moe is not None
    has_prev = kv_prev is not None
    row = lambda w: pl.BlockSpec((tm, w), lambda i: (i, 0))
    fmajor = pl.BlockSpec((H_A, tm // tq, LANES, tq), lambda i: (0, i, 0, 0))
    kv_rows = pl.BlockSpec((None, tm * H_A, LANES), lambda i: (layer, i, 0))
    in_specs = [row(D_MODEL)] + ([row(D_MODEL)] if has_moe else []) + [_full((1, D_MODEL)), _full((D_MODEL, D_IN))]
    args = [x] + ([y_moe] if has_moe else []) + [g.reshape(1, D_MODEL), w_bf]
    aliases = {}
    n_lead_out = 1 if has_moe else 0
    if has_prev:
        aliases = {len(args): n_lead_out + 1, len(args) + 1: n_lead_out + 2}
        in_specs += [_ANY, _ANY]
        args += list(kv_prev)
    fm_shape = jax.ShapeDtypeStruct((H_A, n // tq, LANES, tq), BF16)
    kv_shape = jax.ShapeDtypeStruct((depth, n * H_A, LANES), F32)
    outs = [
        (fm_shape, fmajor),
        (kv_shape, kv_rows),
        (kv_shape, kv_rows),
        (jax.ShapeDtypeStruct((n, W_A), BF16), row(W_A)),
        (fm_shape, fmajor),
        (jax.ShapeDtypeStruct((n, W_B), F32), row(W_B)),
        (jax.ShapeDtypeStruct((n, W_C), F32), row(W_C)),
        (jax.ShapeDtypeStruct((n, W_C), F32), row(W_C)),
    ]
    if has_moe:
        outs = [(jax.ShapeDtypeStruct((n, D_MODEL), F32), row(D_MODEL))] + outs
    res = pl.pallas_call(
        functools.partial(_inproj_prompt_body, has_moe=has_moe, has_prev=has_prev, tm=tm, tq=tq),
        grid=(n // tm,),
        in_specs=in_specs,
        out_specs=[s for _, s in outs],
        out_shape=[s for s, _ in outs],
        input_output_aliases=aliases,
        compiler_params=_cparams(("parallel",)),
        name="inproj_prompt",
    )(*args)
    if has_moe:
        return res
    return [x] + list(res)


def _inproj_sample_body(*refs, has_moe):
    refs = list(refs)
    x = refs.pop(0)[...]
    if has_moe:
        x = x + refs.pop(0)[...]
    g_ref, w_ref = refs.pop(0), refs.pop(0)
    if has_moe:
        refs.pop(0)[...] = x
    q_ref, k_ref, v_ref, p_ref, u_ref, gv_ref = refs
    xn = _rms(x, g_ref[...], NORM_EPS)

    def seg(lo, hi):
        return jnp.dot(xn, w_ref[:, lo:hi], preferred_element_type=F32, precision=HIGHEST)

    q_ref[...] = (seg(0, W_A) * Q_SCALE).astype(BF16)
    k_ref[...] = seg(W_A, 2 * W_A)
    v_ref[...] = seg(2 * W_A, 3 * W_A)
    o = 3 * W_A
    p_ref[...] = seg(o, o + W_B)
    u_ref[...] = seg(o + W_B, o + W_B + W_C)
    gv_ref[...] = seg(o + W_B + W_C, D_IN)


def _inproj_sample(x, y_moe, g, w_f32):
    n = x.shape[0]
    has_moe = y_moe is not None
    blk = lambda w: _full((n, w))
    outs = [(jax.ShapeDtypeStruct((n, W_A), BF16), blk(W_A)), (jax.ShapeDtypeStruct((n, W_A), F32), blk(W_A)),
            (jax.ShapeDtypeStruct((n, W_A), F32), blk(W_A)), (jax.ShapeDtypeStruct((n, W_B), F32), blk(W_B)),
            (jax.ShapeDtypeStruct((n, W_C), F32), blk(W_C)), (jax.ShapeDtypeStruct((n, W_C), F32), blk(W_C))]
    if has_moe:
        outs = [(jax.ShapeDtypeStruct((n, D_MODEL), F32), blk(D_MODEL))] + outs
    args = [x] + ([y_moe] if has_moe else []) + [g.reshape(1, D_MODEL), w_f32]
    res = pl.pallas_call(
        functools.partial(_inproj_sample_body, has_moe=has_moe),
        grid=(1,),
        in_specs=[blk(D_MODEL)] * (2 if has_moe else 1) + [_full((1, D_MODEL)), _full((D_MODEL, D_IN))],
        out_specs=[s for _, s in outs],
        out_shape=[s for s, _ in outs],
        compiler_params=_cparams(("arbitrary",)),
        name="inproj_sample",
    )(*args)
    if has_moe:
        return res
    return [x] + list(res)


def _diff_lambda(lq1, lk1, lq2, lk2, li):
    s1 = jnp.sum(lq1[...] * lk1[...], axis=-1, keepdims=True)
    s2 = jnp.sum(lq2[...] * lk2[...], axis=-1, keepdims=True)
    return jnp.exp(s1) - jnp.exp(s2) + li


def _split_maps(q, axis):
    idx = lax.broadcasted_iota(jnp.int32, q.shape, axis)
    zero = jnp.zeros_like(q)
    return jnp.where(idx < HD_A, q, zero), jnp.where(idx >= HD_A, q, zero)


def _attn_prompt_body(qt_ref, k_ref, vt_ref, lq1, lk1, lq2, lk2, sg_ref, o_ref, *, tq, nq, li):
    i = pl.program_id(2)
    qz = _split_maps(qt_ref[...], 0)
    lam = _diff_lambda(lq1, lk1, lq2, lk2, li)

    def step(j, state, masked):
        kb = k_ref[j * tq:(j + 1) * tq, :]
        vt = vt_ref[j]
        ss = [jnp.dot(kb, qz[m], preferred_element_type=F32) for m in range(2)]
        if masked:
            key = lax.broadcasted_iota(jnp.int32, ss[0].shape, 0)
            qry = lax.broadcasted_iota(jnp.int32, ss[0].shape, 1)
            ss = [jnp.where(key <= qry, s, -jnp.inf) for s in ss]
        out = []
        for m in range(2):
            m_old, l_old, a_old = state[m]
            m_new = jnp.max(ss[m], axis=0, keepdims=True)
            if m_old is not None:
                m_new = jnp.maximum(m_old, m_new)
            p = jnp.exp2(ss[m] - m_new)
            l_new = jnp.sum(p, axis=0, keepdims=True)
            a_new = jnp.dot(vt, p.astype(BF16), preferred_element_type=F32)
            if m_old is not None:
                alpha = jnp.exp2(m_old - m_new)
                l_new = alpha * l_old + l_new
                a_new = alpha * a_old + a_new
            out.append((m_new, l_new, a_new))
        return out

    for c in range(nq):
        @pl.when(i == c)
        def _():
            state = [(None, None, None)] * 2
            for j in range(c + 1):
                state = step(j, state, j == c)
            (_, l0, a0), (_, l1, a1) = state
            o = a0 / l0 - lam * (a1 / l1)
            r = _rms(o, sg_ref[...], SUBLN_EPS, axis=0) * (1.0 - li)
            o_ref[...] = r.T.astype(BF16)


def _attn_prompt(qt, kb, vt, lam_params, sg_col, li, batch, seq):
    tq = TQ_ATTN
    nq = seq // tq
    q_blk = pl.BlockSpec((None, None, LANES, tq), lambda b, h, i: (h, b * nq + i, 0, 0))
    k_blk = pl.BlockSpec((seq, LANES), lambda b, h, i: (b, h))
    v_blk = pl.BlockSpec((None, nq, LANES, tq), lambda b, h, i: (h, b, 0, 0))
    return pl.pallas_call(
        functools.partial(_attn_prompt_body, tq=tq, nq=nq, li=li),
        grid=(batch, H_A, nq),
        in_specs=[q_blk, k_blk, v_blk] + [_full((1, HD_A))] * 4 + [_full((2 * HD_A, 1))],
        out_specs=pl.BlockSpec((tq, LANES), lambda b, h, i: (b * nq + i, h)),
        out_shape=jax.ShapeDtypeStruct((batch * seq, W_A), BF16),
        compiler_params=_cparams(("parallel", "parallel", "arbitrary")),
        name="attn_prompt",
    )(qt, kb, vt, *lam_params, sg_col)


def _attn_sample_body(pt_ref, q_ref, kn_ref, vn_ref, ck_hbm, cv_hbm, lq1, lk1, lq2, lk2, sg_ref, o_ref,
                      kbuf, vbuf, sem, m_sc, l_sc, a_sc, *, layer, n_pages_step, n_tok, li):
    s_id, c = pl.program_id(0), pl.program_id(1)
    n_chunks = pl.num_programs(1)
    n_steps = pl.num_programs(0) * n_chunks
    t = s_id * n_chunks + c
    nq = n_tok * H_A
    ahead = PAGE_BUFFERS - 1

    def page_copies(step, slot):
        seq = lax.div(step, n_chunks)
        first = lax.rem(step, n_chunks) * n_pages_step
        out = []
        for j in range(n_pages_step):
            page = pt_ref[seq, first + j]
            out.append(pltpu.make_async_copy(ck_hbm.at[layer, page], kbuf.at[slot, j], sem.at[slot, 0]))
            out.append(pltpu.make_async_copy(cv_hbm.at[layer, page], vbuf.at[slot, j], sem.at[slot, 1]))
        return out

    def fetch(step):
        for cp in page_copies(step, lax.rem(step, PAGE_BUFFERS)):
            cp.start()

    @pl.when(t == 0)
    def _():
        for d in range(ahead):
            @pl.when(d < n_steps)
            def _():
                fetch(jnp.int32(d))

    @pl.when(t + ahead < n_steps)
    def _():
        fetch(t + ahead)

    slot = lax.rem(t, PAGE_BUFFERS)
    pltpu.make_async_copy(ck_hbm.at[layer, pl.ds(0, n_pages_step)], kbuf.at[slot], sem.at[slot, 0]).wait()
    pltpu.make_async_copy(cv_hbm.at[layer, pl.ds(0, n_pages_step)], vbuf.at[slot], sem.at[slot, 1]).wait()

    @pl.when(c == 0)
    def _():
        m_sc[...] = jnp.full(m_sc.shape, -jnp.inf, F32)
        l_sc[...] = jnp.zeros(l_sc.shape, F32)
        a_sc[...] = jnp.zeros(a_sc.shape, F32)

    q1, q2 = _split_maps(q_ref[...], 1)
    qb = jnp.concatenate([q1, q2], axis=0)

    def accumulate(blocks):
        ss = []
        for k, _, mask in blocks:
            s = lax.dot_general(qb, k.astype(BF16), (((1,), (1,)), ((), ())), preferred_element_type=F32)
            ss.append(jnp.where(mask, s, -jnp.inf))
        m_old = m_sc[...]
        m_new = m_old
        for s in ss:
            m_new = jnp.maximum(m_new, jnp.max(s, axis=-1, keepdims=True))
        alpha = jnp.exp2(m_old - m_new)
        l_new = alpha * l_sc[...]
        a_new = alpha * a_sc[...]
        for s, (_, v, _) in zip(ss, blocks):
            p = jnp.exp2(s - m_new)
            l_new = l_new + jnp.sum(p, axis=-1, keepdims=True)
            a_new = a_new + jnp.dot(p.astype(BF16), v.astype(BF16), preferred_element_type=F32)
        m_sc[...] = m_new
        l_sc[...] = l_new
        a_sc[...] = a_new

    n_rows = PAGE_SIZE * H_A
    r = lax.broadcasted_iota(jnp.int32, (2 * nq, n_rows), 0)
    col = lax.broadcasted_iota(jnp.int32, (2 * nq, n_rows), 1)
    head_of = lambda t: t & (H_A - 1)
    head_mask = head_of(r) == head_of(col)
    accumulate([(kbuf[slot, j], vbuf[slot, j], head_mask) for j in range(n_pages_step)])

    @pl.when(c == pl.num_programs(1) - 1)
    def _():
        rn = lax.broadcasted_iota(jnp.int32, (2 * nq, nq), 0)
        cn = lax.broadcasted_iota(jnp.int32, (2 * nq, nq), 1)
        q_row = jnp.where(rn >= nq, rn - nq, rn)
        new_mask = (head_of(rn) == head_of(cn)) & (cn <= q_row)
        accumulate([(kn_ref[...], vn_ref[...], new_mask)])
        lam = _diff_lambda(lq1, lk1, lq2, lk2, li)
        o = a_sc[...] / l_sc[...]
        o_ref[...] = _rms(o[:nq] - lam * o[nq:], sg_ref[...], SUBLN_EPS) * (1.0 - li)


def _attn_sample(q, k_new, v_new, cache_k, cache_v, page_table, layer, lam_params, sg, li, n_tok):
    n_seq, n_pages = page_table.shape
    pps = math.gcd(PAGES_PER_STEP, n_pages)
    nq = n_tok * H_A
    n_rows = PAGE_SIZE * H_A

    seq_blk = pl.BlockSpec((None, nq, LANES), lambda s, c, pt: (s, 0, 0))
    const = lambda shape: pl.BlockSpec(shape, lambda s, c, pt: (0,) * len(shape))
    page_buf = pltpu.VMEM((PAGE_BUFFERS, pps, n_rows, LANES), F32)
    grid_spec = pltpu.PrefetchScalarGridSpec(
        num_scalar_prefetch=1,
        grid=(n_seq, n_pages // pps),
        in_specs=[seq_blk, seq_blk, seq_blk, _ANY, _ANY] + [const((1, HD_A))] * 4 + [const((1, 2 * HD_A))],
        out_specs=seq_blk,
        scratch_shapes=[page_buf, page_buf, pltpu.SemaphoreType.DMA((PAGE_BUFFERS, 2)),
                        pltpu.VMEM((2 * nq, 1), F32), pltpu.VMEM((2 * nq, 1), F32), pltpu.VMEM((2 * nq, LANES), F32)],
    )
    return pl.pallas_call(
        functools.partial(_attn_sample_body, layer=layer, n_pages_step=pps, n_tok=n_tok, li=li),
        grid_spec=grid_spec,
        out_shape=jax.ShapeDtypeStruct((n_seq, nq, LANES), F32),
        compiler_params=_cparams(("arbitrary", "arbitrary")),
        name="attn_sample",
    )(page_table, q, k_new, v_new, cache_k, cache_v, *lam_params, sg)


def _group_select(vals):
    lane = lax.broadcasted_iota(jnp.int32, vals[0].shape, 1)
    out = vals[-1]
    for g in range(len(vals) - 2, -1, -1):
        out = jnp.where(lane < (g + 1) * CG_B, vals[g], out)
    return out


def _router(xn, wr_hi, wr_lo, br):
    hi = xn.astype(BF16)
    lo = (xn - hi.astype(F32)).astype(BF16)
    logits = (jnp.dot(hi, wr_hi[...], preferred_element_type=F32)
              + jnp.dot(lo, wr_hi[...], preferred_element_type=F32)
              + jnp.dot(hi, wr_lo[...], preferred_element_type=F32)) + br[...]
    lane = lax.broadcasted_iota(jnp.int32, logits.shape, 1).astype(F32)
    neg = -jnp.inf
    big = float(LANES)

    def first_argmax(vals):
        top = jnp.max(vals, axis=-1, keepdims=True)
        idx = jnp.min(jnp.where(vals == top, lane, big), axis=-1, keepdims=True)
        return top, idx

    is_group = lane < N_GROUPS
    g_top, g_idx = first_argmax(jnp.where(is_group, logits, neg))
    g_w = 1.0 / jnp.sum(jnp.where(is_group, jnp.exp(logits - g_top), 0.0), axis=-1, keepdims=True)
    e_base = N_GROUPS + EXP_PER_GROUP * g_idx
    in_group = (lane >= e_base) & (lane < e_base + EXP_PER_GROUP)
    e_vals = jnp.where(in_group, logits, neg)
    e1, i1 = first_argmax(e_vals)
    e2, i2 = first_argmax(jnp.where(lane == i1, neg, e_vals))
    t = jnp.exp(e2 - e1)
    w1 = g_w / (1.0 + t)
    w2 = g_w * t / (1.0 + t)
    a1 = i1 - e_base
    a2 = i2 - e_base
    lo_id = jnp.minimum(a1, a2)
    hi_id = jnp.maximum(a1, a2)
    w_lo = jnp.where(a1 < a2, w1, w2)
    w_hi = jnp.where(a1 < a2, w2, w1)
    pair_base = jnp.where(lo_id == 0.0, 0.0, jnp.where(lo_id == 1.0, 3.0, 5.0))
    bucket = g_idx * N_PAIRS + pair_base + (hi_id - lo_id - 1.0)
    return jnp.where(lane == 0.0, bucket, jnp.where(lane == 1.0, w_lo, jnp.where(lane == 2.0, w_hi, 0.0)))


def _mix_tail(a, b_out, c_out, x, wout_ref, gffn_ref, wr_hi, wr_lo, br, y_ref, xg_ref, rt_ref, cnt_ref, *, exact):
    if exact:
        mm = lambda t, lo, hi: jnp.dot(t, wout_ref[lo:hi, :], preferred_element_type=F32, precision=HIGHEST)
    else:
        mm = lambda t, lo, hi: jnp.dot(t.astype(BF16), wout_ref[lo:hi, :], preferred_element_type=F32)
    y = x + (mm(a, 0, W_A) + mm(b_out, W_A, W_A + W_B) + mm(c_out, W_A + W_B, D_MODEL))
    y_ref[...] = y
    xn = _rms(y, gffn_ref[...], NORM_EPS)
    route = _router(xn, wr_hi, wr_lo, br)
    xg_ref[:, 0:D_MODEL] = xn
    xg_ref[:, D_MODEL:] = route
    rt_ref[...] = route.T[0:ROUTE_FIELDS, :]
    lane = lax.broadcasted_iota(jnp.int32, route.shape, 1).astype(F32)
    hits = jnp.sum(jnp.where(lane == route[:, 0:1], 1.0, 0.0), axis=0, keepdims=True)

    @pl.when(pl.program_id(0) == 0)
    def _():
        cnt_ref[...] = jnp.zeros(cnt_ref.shape, F32)

    cnt_ref[...] += hits


def _sgu_norm(gv, sgu_g, sgu_b):
    gg = _gelu(gv)
    mu = jnp.mean(gg, axis=-1, keepdims=True)
    d = gg - mu
    var = jnp.mean(d * d, axis=-1, keepdims=True)
    return d * lax.rsqrt(var + NORM_EPS) * sgu_g + sgu_b


def _mix_prompt_body(a_ref, p_ref, halo_ref, u_ref, gv_ref, x_ref, wpool_ref, pscale_ref, sgug_ref, sgub_ref,
                     ws_ref, bs_ref, wout_ref, gffn_ref, wr_hi, wr_lo, br,
                     y_ref, xg_ref, rt_ref, cnt_ref, ext_sc, *, tm, tiles_per_seq):
    i = pl.program_id(0)
    t_in_seq = i % tiles_per_seq
    p = p_ref[...]
    ext_sc[0:POOL_HALO, :] = jnp.where(t_in_seq == 0, 0.0, halo_ref[...])
    ext_sc[POOL_HALO:, :] = p
    pos = t_in_seq * tm + lax.broadcasted_iota(jnp.int32, (tm, 1), 0)
    acc = p
    means = []
    for k in range(1, max(POOL_WINDOWS)):
        acc = acc + ext_sc[POOL_HALO - k:POOL_HALO - k + tm, :]
        if k + 1 in POOL_WINDOWS:
            cnt = jnp.minimum(pos + 1, k + 1).astype(F32)
            means.append(acc / cnt)
    d = (_group_select(means) - p).astype(BF16)
    b_out = jnp.dot(d, wpool_ref[...], preferred_element_type=F32) * pscale_ref[...]

    vn = _sgu_norm(gv_ref[...], sgug_ref[...], sgub_ref[...]).astype(BF16)
    r = lax.broadcasted_iota(jnp.int32, (CHUNK, CHUNK), 0)
    c = lax.broadcasted_iota(jnp.int32, (CHUNK, CHUNK), 1)
    w_tril = [jnp.where(r >= c, ws_ref[h], 0.0).astype(BF16) for h in range(H_C)]
    chunks = []
    for ci in range(tm // CHUNK):
        vc = vn[ci * CHUNK:(ci + 1) * CHUNK, :]
        per_head = [jnp.dot(w_tril[h], vc, preferred_element_type=F32) for h in range(H_C)]
        chunks.append(_group_select(per_head) + bs_ref[...])
    mixed = jnp.concatenate(chunks, axis=0)
    c_out = _gelu(u_ref[...]) * mixed

    _mix_tail(a_ref[...], b_out, c_out, x_ref[...], wout_ref, gffn_ref, wr_hi, wr_lo, br,
              y_ref, xg_ref, rt_ref, cnt_ref, exact=False)


def _mix_out_specs(n, tm):
    row = lambda w: pl.BlockSpec((tm, w), lambda i: (i, 0))
    specs = [row(D_MODEL), row(D_ROUTED), pl.BlockSpec((ROUTE_FIELDS, tm), lambda i: (0, i)), _full((1, LANES))]
    shapes = [jax.ShapeDtypeStruct((n, D_MODEL), F32), jax.ShapeDtypeStruct((n, D_ROUTED), F32),
              jax.ShapeDtypeStruct((ROUTE_FIELDS, n), F32), jax.ShapeDtypeStruct((1, LANES), F32)]
    return specs, shapes


def _mix_prompt(a, p, u, gv, x, lw, seq):
    n = x.shape[0]
    tm = TM_MIX
    tiles_per_seq = seq // tm
    halo_per_tile = tm // POOL_HALO
    row = lambda w: pl.BlockSpec((tm, w), lambda i: (i, 0))
    halo = pl.BlockSpec((POOL_HALO, W_B), lambda i: (jnp.maximum(i * halo_per_tile - 1, 0), 0))
    out_specs, out_shape = _mix_out_specs(n, tm)
    return pl.pallas_call(
        functools.partial(_mix_prompt_body, tm=tm, tiles_per_seq=tiles_per_seq),
        grid=(n // tm,),
        in_specs=[row(W_A), row(W_B), halo, row(W_C), row(W_C), row(D_MODEL),
                  _full((W_B, W_B)), _full((1, W_B)), _full((1, W_C)), _full((1, W_C)),
                  _full((H_C, CHUNK, CHUNK)), _full((CHUNK, W_C)), _full((D_MODEL, D_MODEL)),
                  _full((1, D_MODEL)), _full((D_MODEL, LANES)), _full((D_MODEL, LANES)), _full((1, LANES))],
        out_specs=out_specs,
        out_shape=out_shape,
        scratch_shapes=[pltpu.VMEM((POOL_HALO + tm, W_B), F32)],
        compiler_params=_cparams(("arbitrary",)),
        name="mix_prompt",
    )(a, p, p, u, gv, x, lw["wpool"], lw["pscale"], lw["sgu_g"], lw["sgu_b"], lw["w_s"], lw["bs_chunk"],
      lw["w_out"], lw["g_ffn"], lw["wr_hi"], lw["wr_lo"], lw["br"])


def _mix_sample_body(a_ref, pext_ref, u_ref, gv_ref, x_ref, wpool_ref, pscale_ref, sgug_ref, sgub_ref,
                     wl_ref, bl_ref, wout_ref, gffn_ref, wr_hi, wr_lo, br,
                     y_ref, xg_ref, rt_ref, cnt_ref, vn_ref, *, n_tok, n_seq):
    ds = []
    for i in range(n_tok):
        cur = pext_ref[POOL_BUF + i]
        acc = cur
        means = []
        for k in range(1, max(POOL_WINDOWS)):
            acc = acc + pext_ref[POOL_BUF + i - k]
            if k + 1 in POOL_WINDOWS:
                means.append(acc / float(k + 1))
        ds.append(_group_select(means) - cur)
    d = jnp.concatenate(ds, axis=0)
    b_out = jnp.dot(d, wpool_ref[...], preferred_element_type=F32, precision=HIGHEST) * pscale_ref[...]

    vn = _sgu_norm(gv_ref[...], sgug_ref[...], sgub_ref[...])
    vn_ref[...] = vn
    rows = []
    for i in range(n_tok):
        acc = jnp.broadcast_to(bl_ref[i:i + 1, :], (n_seq, W_C))
        for j in range(i + 1):
            acc = acc + wl_ref[i * n_tok + j:i * n_tok + j + 1, :] * vn[j * n_seq:(j + 1) * n_seq, :]
        rows.append(acc)
    c_out = _gelu(u_ref[...]) * jnp.concatenate(rows, axis=0)

    _mix_tail(a_ref[...], b_out, c_out, x_ref[...], wout_ref, gffn_ref, wr_hi, wr_lo, br,
              y_ref, xg_ref, rt_ref, cnt_ref, exact=True)


def _mix_sample(a, pext, u, gv, x, lw, n_tok, n_seq):
    n = x.shape[0]
    out_specs, out_shape = _mix_out_specs(n, n)
    return pl.pallas_call(
        functools.partial(_mix_sample_body, n_tok=n_tok, n_seq=n_seq),
        grid=(1,),
        in_specs=[_full((n, W_A)), _full((POOL_BUF + n_tok, n_seq, W_B)), _full((n, W_C)), _full((n, W_C)),
                  _full((n, D_MODEL)), _full((W_B, W_B)), _full((1, W_B)), _full((1, W_C)), _full((1, W_C)),
                  _full((n_tok * n_tok, W_C)), _full((n_tok, W_C)), _full((D_MODEL, D_MODEL)),
                  _full((1, D_MODEL)), _full((D_MODEL, LANES)), _full((D_MODEL, LANES)), _full((1, LANES))],
        out_specs=out_specs + [_full((n, W_C))],
        out_shape=out_shape + [jax.ShapeDtypeStruct((n, W_C), F32)],
        compiler_params=_cparams(("arbitrary",)),
        name="mix_sample",
    )(a, pext, u, gv, x, lw["wpool_f32"], lw["pscale"], lw["sgu_g"], lw["sgu_b"], lw["wl_new"], lw["bl_new"],
      lw["w_out_f32"], lw["g_ffn"], lw["wr_hi"], lw["wr_lo"], lw["br"])


def _moe_body(src_ref, elo_ref, ehi_ref, nv_ref, nact_ref, x_hbm, wg_lo, wg_hi, wu_lo, wu_hi, wd_lo, wd_hi,
              y_hbm, xbuf, ybuf, gsem, ssem, *, tm):
    i = pl.program_id(0)
    n_act = nact_ref[0]
    slot = lax.rem(i, 2)

    def row_copy_in(tile, s, r):
        return pltpu.make_async_copy(x_hbm.at[pl.ds(src_ref[tile * tm + r], 1)], xbuf.at[s, pl.ds(r, 1)], gsem.at[s])

    def row_copy_out(tile, s, r):
        return pltpu.make_async_copy(ybuf.at[s, pl.ds(r, 1)], y_hbm.at[pl.ds(src_ref[tile * tm + r], 1)], ssem.at[s])

    def start_gather(tile, s, unrolled):
        if unrolled:
            for r in range(tm):
                row_copy_in(tile, s, r).start()
            return

        def one(r, c):
            row_copy_in(tile, s, r).start()
            return c
        lax.fori_loop(0, tm, one, 0, unroll=8)

    def wait_gather(s):
        pltpu.make_async_copy(x_hbm.at[pl.ds(0, tm)], xbuf.at[s], gsem.at[s]).wait()

    def whole_groups(n):
        return pl.multiple_of(lax.shift_left(lax.shift_right_logical(n, 3), 3), SUBLANES)

    def start_scatter(tile, s):
        n = nv_ref[tile]

        @pl.when(n == tm)
        def _():
            for r in range(tm):
                row_copy_out(tile, s, r).start()

        @pl.when(n < tm)
        def _():
            def group(g, c):
                for u in range(SUBLANES):
                    row_copy_out(tile, s, g * SUBLANES + u).start()
                return c
            lax.fori_loop(0, lax.shift_right_logical(n, 3), group, 0)

            def one(r, c):
                row_copy_out(tile, s, r).start()
                return c
            lax.fori_loop(whole_groups(n), n, one, 0)

    def wait_scatter(tile, s):
        n = nv_ref[tile]
        n_blk = whole_groups(n)

        @pl.when(n_blk > 0)
        def _():
            pltpu.make_async_copy(ybuf.at[s, pl.ds(0, n_blk)], y_hbm.at[pl.ds(0, n_blk)], ssem.at[s]).wait()

        def one(r, c):
            pltpu.make_async_copy(ybuf.at[s, pl.ds(0, 1)], y_hbm.at[pl.ds(0, 1)], ssem.at[s]).wait()
            return c
        lax.fori_loop(n_blk, n, one, 0)

    @pl.when(i == 0)
    def _():
        start_gather(0, 0, False)

    @pl.when(i + 1 < n_act)
    def _():
        start_gather(i + 1, 1 - slot, True)

    @pl.when(i < n_act)
    def _():
        wait_gather(slot)

        @pl.when(i >= 2)
        def _():
            wait_scatter(i - 2, slot)

        xrow = xbuf[slot]
        x = xrow[:, 0:D_MODEL].astype(BF16)
        gate_lo = xrow[:, D_MODEL + 1:D_MODEL + 2]
        gate_hi = xrow[:, D_MODEL + 2:D_MODEL + 3]

        def hidden(wg, wu, gate):
            hg = jnp.dot(x, wg[...], preferred_element_type=F32)
            hu = jnp.dot(x, wu[...], preferred_element_type=F32)
            return (hg / (1.0 + jnp.exp(-hg)) * hu * gate).astype(BF16)

        h_lo = hidden(wg_lo, wu_lo, gate_lo)
        h_hi = hidden(wg_hi, wu_hi, gate_hi)
        ybuf[slot] = (jnp.dot(h_lo, wd_lo[...], preferred_element_type=F32)
                      + jnp.dot(h_hi, wd_hi[...], preferred_element_type=F32))
        start_scatter(i, slot)

        @pl.when(i == n_act - 1)
        def _():
            wait_scatter(i, slot)

            @pl.when(i >= 1)
            def _():
                wait_scatter(i - 1, 1 - slot)


def _moe(xg, src, e_lo, e_hi, n_valid, n_active, wg, wu, wd, tm):
    n = xg.shape[0]
    n_tiles = e_lo.shape[0]
    wspec = lambda shape, which: pl.BlockSpec(
        (None,) + shape, lambda i, src, elo, ehi, nv, na: ((elo, ehi)[which][i], 0, 0))
    up, dn = (D_MODEL, D_FF), (D_FF, D_MODEL)
    grid_spec = pltpu.PrefetchScalarGridSpec(
        num_scalar_prefetch=5,
        grid=(n_tiles,),
        in_specs=[_ANY, wspec(up, 0), wspec(up, 1), wspec(up, 0), wspec(up, 1), wspec(dn, 0), wspec(dn, 1)],
        out_specs=_ANY,
        scratch_shapes=[pltpu.VMEM((2, tm, D_ROUTED), F32), pltpu.VMEM((2, tm, D_MODEL), F32),
                        pltpu.SemaphoreType.DMA((2,)), pltpu.SemaphoreType.DMA((2,))],
    )
    return pl.pallas_call(
        functools.partial(_moe_body, tm=tm),
        grid_spec=grid_spec,
        out_shape=jax.ShapeDtypeStruct((n, D_MODEL), F32),
        compiler_params=_cparams(("arbitrary",)),
        name="moe",
    )(src, e_lo, e_hi, n_valid, n_active, xg, wg, wg, wu, wu, wd, wd)


_PAIR_LO = (0, 0, 0, 1, 1, 2)
_PAIR_HI = (1, 2, 3, 2, 3, 3)


def _moe_dispatch(xg, route_t, cnt, wg, wu, wd, tm):
    n = xg.shape[0]
    n_tiles = -(-n // tm) + N_BUCKETS
    bucket = route_t[0].astype(jnp.int32)
    order = jnp.argsort(bucket, stable=True).astype(jnp.int32)
    counts = cnt[0, :N_BUCKETS].astype(jnp.int32)
    padded = ((counts + tm - 1) // tm) * tm
    ends = jnp.cumsum(padded)
    starts = ends - padded
    sorted_starts = jnp.cumsum(counts) - counts
    tile_start = jnp.arange(n_tiles, dtype=jnp.int32) * tm
    tile_bucket = jnp.minimum(jnp.sum(tile_start[:, None] >= ends[None, :], axis=1, dtype=jnp.int32), N_BUCKETS - 1)
    in_bucket = tile_start - starts[tile_bucket]
    n_valid = jnp.where(tile_start < ends[-1], jnp.clip(counts[tile_bucket] - in_bucket, 0, tm), 0).astype(jnp.int32)
    r_in = jnp.arange(tm, dtype=jnp.int32)[None, :]
    sorted_idx = (sorted_starts[tile_bucket] + in_bucket)[:, None] + r_in
    src = jnp.where(r_in < n_valid[:, None], order[jnp.clip(sorted_idx, 0, n - 1)], 0).reshape(-1)
    group = tile_bucket // N_PAIRS
    pair = tile_bucket % N_PAIRS
    e_lo = group * EXP_PER_GROUP + jnp.asarray(_PAIR_LO, jnp.int32)[pair]
    e_hi = group * EXP_PER_GROUP + jnp.asarray(_PAIR_HI, jnp.int32)[pair]
    n_active = (ends[-1] // tm).reshape(1).astype(jnp.int32)
    return _moe(xg, src.astype(jnp.int32), e_lo, e_hi, n_valid, n_active, wg, wu, wd, tm)


def _final_body(x_ref, y_ref, g_ref, o_ref):
    o_ref[...] = _rms(x_ref[...] + y_ref[...], g_ref[...], NORM_EPS)


def _final_norm(x, y_moe, g, tm):
    n = x.shape[0]
    row = pl.BlockSpec((tm, D_MODEL), lambda i: (i, 0))
    return pl.pallas_call(
        _final_body,
        grid=(n // tm,),
        in_specs=[row, row, _full((1, D_MODEL))],
        out_specs=row,
        out_shape=jax.ShapeDtypeStruct((n, D_MODEL), F32),
        compiler_params=_cparams(("parallel",)),
        name="final_norm",
    )(x, y_moe, g.reshape(1, D_MODEL))


def _layer_weights(l, n_tok, norm_ffn_g, w_out, subln_g, w_pool, pool_scale, sgu_norm_g, sgu_norm_b,
                   w_spatial, b_spatial, w_router_group, b_router_group, w_router_expert, b_router_expert):
    wpool = jnp.zeros((W_B, W_B), F32)
    for g in range(len(POOL_WINDOWS)):
        wpool = wpool.at[g * CG_B:(g + 1) * CG_B, g * CG_B:(g + 1) * CG_B].set(w_pool[l, g])
    wr = jnp.concatenate([w_router_group[l], w_router_expert[l],
                          jnp.zeros((D_MODEL, LANES - N_GROUPS - N_EXPERTS), F32)], axis=1)
    wr_hi = wr.astype(BF16)
    br = jnp.concatenate([b_router_group[l], b_router_expert[l],
                          jnp.zeros((LANES - N_GROUPS - N_EXPERTS,), F32)]).reshape(1, LANES)
    per_lane = lambda t: jnp.repeat(t, HD_C, axis=-1)
    ws_new = jnp.tril(w_spatial[l, :, :n_tok, :n_tok])
    return dict(
        wpool=wpool.astype(BF16), wpool_f32=wpool, pscale=pool_scale[l].reshape(1, W_B),
        sgu_g=sgu_norm_g[l].reshape(1, W_C), sgu_b=sgu_norm_b[l].reshape(1, W_C),
        w_s=w_spatial[l], bs_chunk=per_lane(b_spatial[l].T),
        wl_new=per_lane(jnp.transpose(ws_new, (1, 2, 0)).reshape(n_tok * n_tok, H_C)),
        bl_new=per_lane(b_spatial[l, :, :n_tok].T),
        w_out=w_out[l].astype(BF16), w_out_f32=w_out[l], g_ffn=norm_ffn_g[l].reshape(1, D_MODEL),
        wr_hi=wr_hi, wr_lo=(wr - wr_hi.astype(F32)).astype(BF16), br=br,
        sg=subln_g[l].reshape(1, 2 * HD_A), sg_col=subln_g[l].reshape(2 * HD_A, 1),
    )


def kernel(x_prompt, x_sample, cache_k, cache_v, state_pool, page_table, norm_mix_g, norm_ffn_g, w_in, w_out, lambda_q1, lambda_k1, lambda_q2, lambda_k2, subln_g, w_pool, pool_scale, sgu_norm_g, sgu_norm_b, w_spatial, b_spatial, w_router_group, b_router_group, w_router_expert, b_router_expert, w_exp_gate, w_exp_up, w_exp_down, final_norm_g):
    batch, seq, _ = x_prompt.shape
    n_seq, n_tok, _ = x_sample.shape
    depth = w_in.shape[0]
    n_p = batch * seq
    n_s = n_seq * n_tok
    assert seq % TM_MIX == 0 and seq % TQ_ATTN == 0 and n_p % TM_INPROJ == 0 and TM_INPROJ % TQ_ATTN == 0
    assert n_s % SUBLANES == 0 and page_table.shape[1] * PAGE_SIZE >= max(POOL_WINDOWS)

    xp = x_prompt.reshape(n_p, D_MODEL)
    xs = jnp.transpose(x_sample, (1, 0, 2)).reshape(n_s, D_MODEL)
    yp_moe = ys_moe = None
    kv_prompt = tuple(jnp.zeros((depth, n_p * H_A, 2 * HD_A), F32) for _ in range(2))
    ck = cache_k.reshape(depth, cache_k.shape[1], PAGE_SIZE * H_A, 2 * HD_A)
    cv = cache_v.reshape(depth, cache_v.shape[1], PAGE_SIZE * H_A, 2 * HD_A)

    def to_seq_major(t, width):
        return jnp.transpose(t.reshape(n_tok, n_seq, width), (1, 0, 2))

    pp_l, ks_l, vs_l, ps_l, cs_l = [], [], [], [], []
    for l in range(depth):
        li = _lambda_init(l)
        lw = _layer_weights(l, n_tok, norm_ffn_g, w_out, subln_g, w_pool, pool_scale, sgu_norm_g, sgu_norm_b,
                            w_spatial, b_spatial, w_router_group, b_router_group, w_router_expert, b_router_expert)
        lam_params = tuple(t[l].reshape(1, HD_A) for t in (lambda_q1, lambda_k1, lambda_q2, lambda_k2))
        wg, wu, wd = w_exp_gate[l].astype(BF16), w_exp_up[l].astype(BF16), w_exp_down[l].astype(BF16)

        xp, qt, k_all, v_all, kb, vt, p, u, gv = _inproj_prompt(
            xp, yp_moe, norm_mix_g[l], w_in[l].astype(BF16), l, depth, kv_prompt)
        kv_prompt = (k_all, v_all)
        a = _attn_prompt(qt, kb, vt, lam_params, lw["sg_col"], li, batch, seq)
        xp, xg, route_t, cnt = _mix_prompt(a, p, u, gv, xp, lw, seq)
        yp_moe = _moe_dispatch(xg, route_t, cnt, wg, wu, wd, TM_MOE)
        pp_l.append(p.reshape(batch, seq, W_B)[:, seq - POOL_BUF:])

        xs, q, k, v, p, u, gv = _inproj_sample(xs, ys_moe, norm_mix_g[l], w_in[l])
        per_seq = lambda t: to_seq_major(t, W_A).reshape(n_seq, n_tok * H_A, 2 * HD_A)
        a = _attn_sample(per_seq(q), per_seq(k), per_seq(v), ck, cv, page_table, l, lam_params, lw["sg"], li, n_tok)
        a = jnp.transpose(a.reshape(n_seq, n_tok, W_A), (1, 0, 2)).reshape(n_s, W_A)
        pext = jnp.concatenate([jnp.transpose(state_pool[l], (1, 0, 2)), p.reshape(n_tok, n_seq, W_B)], axis=0)
        xs, xg, route_t, cnt, vn = _mix_sample(a, pext, u, gv, xs, lw, n_tok, n_seq)
        ys_moe = _moe_dispatch(xg, route_t, cnt, wg, wu, wd, TM_MOE_SAMPLE)
        ks_l.append(to_seq_major(k, W_A).reshape(n_seq, n_tok, H_A, 2 * HD_A))
        vs_l.append(to_seq_major(v, W_A).reshape(n_seq, n_tok, H_A, 2 * HD_A))
        ps_l.append(jnp.transpose(pext[n_tok:], (1, 0, 2)))
        cs_l.append(to_seq_major(vn, W_C))

    y_prompt = _final_norm(xp, yp_moe, final_norm_g, TM_FINAL).reshape(batch, seq, D_MODEL)
    y_sample = to_seq_major(_final_norm(xs, ys_moe, final_norm_g, n_s), D_MODEL)
    k_prompt = kv_prompt[0].reshape(depth, batch, seq, H_A, 2 * HD_A)
    v_prompt = kv_prompt[1].reshape(depth, batch, seq, H_A, 2 * HD_A)
    return (y_prompt, y_sample, k_prompt, v_prompt, jnp.stack(pp_l),
            jnp.stack(ks_l), jnp.stack(vs_l), jnp.stack(ps_l), jnp.stack(cs_l))
```

```python
import functools
import math

import jax
import jax.numpy as jnp
from jax import lax
from jax.experimental import pallas as pl
from jax.experimental.pallas import tpu as pltpu

F32 = jnp.float32
BF16 = jnp.bfloat16
HIGHEST = lax.Precision.HIGHEST

D_MODEL = 1024
H_A = 4
HD_A = 64
W_A = H_A * 2 * HD_A
POOL_WINDOWS = (2, 4, 8, 16)
CG_B = 64
W_B = len(POOL_WINDOWS) * CG_B
POOL_BUF = max(POOL_WINDOWS) - 1
H_C = 4
HD_C = 64
W_C = H_C * HD_C
CHUNK = 128
D_IN = 3 * W_A + W_B + 2 * W_C
N_GROUPS = 4
EXP_PER_GROUP = 4
N_EXPERTS = N_GROUPS * EXP_PER_GROUP
N_PAIRS = 6
N_BUCKETS = N_GROUPS * N_PAIRS
D_FF = D_MODEL // 2
PAGE_SIZE = 128
NORM_EPS = 1e-6
SUBLN_EPS = 1e-5

LANES = 128
SUBLANES = 8
POOL_HALO = 16
VMEM_LIMIT = 56 * 1024 * 1024
D_ROUTED = D_MODEL + LANES
ROUTE_FIELDS = SUBLANES

Q_SCALE = (HD_A ** -0.5) * math.log2(math.e)

TM_INPROJ = 512
TQ_ATTN = 512
TM_MIX = 256
TM_MOE = 256
TM_MOE_SAMPLE = 16
TM_FINAL = 512
PAGES_PER_STEP = 16
PAGE_BUFFERS = 3


def _lambda_init(layer):
    return 0.8 - 0.6 * math.exp(-0.3 * layer)


def _cparams(sem):
    return pltpu.CompilerParams(dimension_semantics=sem, vmem_limit_bytes=VMEM_LIMIT)


def _full(shape):
    return pl.BlockSpec(shape, lambda *_: (0,) * len(shape))


_ANY = pl.BlockSpec(memory_space=pl.ANY)


def _rms(x, g, eps, axis=-1):
    return x * lax.rsqrt(jnp.mean(x * x, axis=axis, keepdims=True) + eps) * g


def _gelu(x):
    return 0.5 * x * (1.0 + lax.erf(x * (2.0 ** -0.5)))


def _inproj_prompt_body(*refs, has_moe, has_prev, tm, tq):
    refs = list(refs)
    x_ref = refs.pop(0)
    x = x_ref[...]
    if has_moe:
        x = x + refs.pop(0)[...]
    g_ref, w_ref = refs.pop(0), refs.pop(0)
    if has_prev:
        refs.pop(0), refs.pop(0)
    if has_moe:
        refs.pop(0)[...] = x
    qt_ref, k_ref, v_ref, kb_ref, vt_ref, p_ref, u_ref, gv_ref = refs
    xn = _rms(x, g_ref[...], NORM_EPS).astype(BF16)

    def seg(lo, hi):
        return jnp.dot(xn, w_ref[:, lo:hi], preferred_element_type=F32)

    def put_feature_major(ref, t):
        tt = t.T.astype(BF16)
        for h in range(H_A):
            for s in range(tm // tq):
                ref[h, s] = tt[h * LANES:(h + 1) * LANES, s * tq:(s + 1) * tq]

    def put_head_rows(ref, t):
        for h in range(H_A):
            ref[pl.ds(h, tm, stride=H_A), :] = t[:, h * LANES:(h + 1) * LANES]

    put_feature_major(qt_ref, seg(0, W_A) * Q_SCALE)
    k = seg(W_A, 2 * W_A)
    put_head_rows(k_ref, k)
    kb_ref[...] = k.astype(BF16)
    v = seg(2 * W_A, 3 * W_A)
    put_head_rows(v_ref, v)
    put_feature_major(vt_ref, v)
    o = 3 * W_A
    p_ref[...] = seg(o, o + W_B)
    u_ref[...] = seg(o + W_B, o + W_B + W_C)
    gv_ref[...] = seg(o + W_B + W_C, D_IN)


def _inproj_prompt(x, y_moe, g, w_bf, layer, depth, kv_prev):
    n = x.shape[0]
    tm, tq = TM_INPROJ, TQ_ATTN
    has_moe = y_moe is not None
    has_prev = kv_prev is not None
    row = lambda w: pl.BlockSpec((tm, w), lambda i: (i, 0))
    fmajor = pl.BlockSpec((H_A, tm // tq, LANES, tq), lambda i: (0, i, 0, 0))
    kv_rows = pl.BlockSpec((None, tm * H_A, LANES), lambda i: (layer, i, 0))
    in_specs = [row(D_MODEL)] + ([row(D_MODEL)] if has_moe else []) + [_full((1, D_MODEL)), _full((D_MODEL, D_IN))]
    args = [x] + ([y_moe] if has_moe else []) + [g.reshape(1, D_MODEL), w_bf]
    aliases = {}
    n_lead_out = 1 if has_moe else 0
    if has_prev:
        aliases = {len(args): n_lead_out + 1, len(args) + 1: n_lead_out + 2}
        in_specs += [_ANY, _ANY]
        args += list(kv_prev)
    fm_shape = jax.ShapeDtypeStruct((H_A, n // tq, LANES, tq), BF16)
    kv_shape = jax.ShapeDtypeStruct((depth, n * H_A, LANES), F32)
    outs = [
        (fm_shape, fmajor),
        (kv_shape, kv_rows),
        (kv_shape, kv_rows),
        (jax.ShapeDtypeStruct((n, W_A), BF16), row(W_A)),
        (fm_shape, fmajor),
        (jax.ShapeDtypeStruct((n, W_B), F32), row(W_B)),
        (jax.ShapeDtypeStruct((n, W_C), F32), row(W_C)),
        (jax.ShapeDtypeStruct((n, W_C), F32), row(W_C)),
    ]
    if has_moe:
        outs = [(jax.ShapeDtypeStruct((n, D_MODEL), F32), row(D_MODEL))] + outs
    res = pl.pallas_call(
        functools.partial(_inproj_prompt_body, has_moe=has_moe, has_prev=has_prev, tm=tm, tq=tq),
        grid=(n // tm,),
        in_specs=in_specs,
        out_specs=[s for _, s in outs],
        out_shape=[s for s, _ in outs],
        input_output_aliases=aliases,
        compiler_params=_cparams(("parallel",)),
        name="inproj_prompt",
    )(*args)
    if has_moe:
        return res
    return [x] + list(res)


def _inproj_sample_body(*refs, has_moe):
    refs = list(refs)
    x = refs.pop(0)[...]
    if has_moe:
        x = x + refs.pop(0)[...]
    g_ref, w_ref = refs.pop(0), refs.pop(0)
    if has_moe:
        refs.pop(0)[...] = x
    q_ref, k_ref, v_ref, p_ref, u_ref, gv_ref = refs
    xn = _rms(x, g_ref[...], NORM_EPS)

    def seg(lo, hi):
        return jnp.dot(xn, w_ref[:, lo:hi], preferred_element_type=F32, precision=HIGHEST)

    q_ref[...] = (seg(0, W_A) * Q_SCALE).astype(BF16)
    k_ref[...] = seg(W_A, 2 * W_A)
    v_ref[...] = seg(2 * W_A, 3 * W_A)
    o = 3 * W_A
    p_ref[...] = seg(o, o + W_B)
    u_ref[...] = seg(o + W_B, o + W_B + W_C)
    gv_ref[...] = seg(o + W_B + W_C, D_IN)


def _inproj_sample(x, y_moe, g, w_f32):
    n = x.shape[0]
    has_moe = y_moe is not None
    blk = lambda w: _full((n, w))
    outs = [(jax.ShapeDtypeStruct((n, W_A), BF16), blk(W_A)), (jax.ShapeDtypeStruct((n, W_A), F32), blk(W_A)),
            (jax.ShapeDtypeStruct((n, W_A), F32), blk(W_A)), (jax.ShapeDtypeStruct((n, W_B), F32), blk(W_B)),
            (jax.ShapeDtypeStruct((n, W_C), F32), blk(W_C)), (jax.ShapeDtypeStruct((n, W_C), F32), blk(W_C))]
    if has_moe:
        outs = [(jax.ShapeDtypeStruct((n, D_MODEL), F32), blk(D_MODEL))] + outs
    args = [x] + ([y_moe] if has_moe else []) + [g.reshape(1, D_MODEL), w_f32]
    res = pl.pallas_call(
        functools.partial(_inproj_sample_body, has_moe=has_moe),
        grid=(1,),
        in_specs=[blk(D_MODEL)] * (2 if has_moe else 1) + [_full((1, D_MODEL)), _full((D_MODEL, D_IN))],
        out_specs=[s for _, s in outs],
        out_shape=[s for s, _ in outs],
        compiler_params=_cparams(("arbitrary",)),
        name="inproj_sample",
    )(*args)
    if has_moe:
        return res
    return [x] + list(res)


def _diff_lambda(lq1, lk1, lq2, lk2, li):
    s1 = jnp.sum(lq1[...] * lk1[...], axis=-1, keepdims=True)
    s2 = jnp.sum(lq2[...] * lk2[...], axis=-1, keepdims=True)
    return jnp.exp(s1) - jnp.exp(s2) + li


def _split_maps(q, axis):
    idx = lax.broadcasted_iota(jnp.int32, q.shape, axis)
    zero = jnp.zeros_like(q)
    return jnp.where(idx < HD_A, q, zero), jnp.where(idx >= HD_A, q, zero)


def _attn_prompt_body(*refs, tq, nq, li):
    _prompt_attn_step(pl.program_id(2), *refs, tq=tq, nq=nq, li=li)


def _prompt_attn_step(i, qt_ref, k_ref, vt_ref, lq1, lk1, lq2, lk2, sg_ref, o_ref, *, tq, nq, li):
    qz = _split_maps(qt_ref[...], 0)
    lam = _diff_lambda(lq1, lk1, lq2, lk2, li)

    def step(j, state, masked):
        kb = k_ref[j * tq:(j + 1) * tq, :]
        vt = vt_ref[j]
        ss = [jnp.dot(kb, qz[m], preferred_element_type=F32) for m in range(2)]
        if masked:
            key = lax.broadcasted_iota(jnp.int32, ss[0].shape, 0)
            qry = lax.broadcasted_iota(jnp.int32, ss[0].shape, 1)
            ss = [jnp.where(key <= qry, s, -jnp.inf) for s in ss]
        out = []
        for m in range(2):
            m_old, l_old, a_old = state[m]
            m_new = jnp.max(ss[m], axis=0, keepdims=True)
            if m_old is not None:
                m_new = jnp.maximum(m_old, m_new)
            p = jnp.exp2(ss[m] - m_new)
            l_new = jnp.sum(p, axis=0, keepdims=True)
            a_new = jnp.dot(vt, p.astype(BF16), preferred_element_type=F32)
            if m_old is not None:
                alpha = jnp.exp2(m_old - m_new)
                l_new = alpha * l_old + l_new
                a_new = alpha * a_old + a_new
            out.append((m_new, l_new, a_new))
        return out

    for c in range(nq):
        @pl.when(i == c)
        def _():
            state = [(None, None, None)] * 2
            for j in range(c + 1):
                state = step(j, state, j == c)
            (_, l0, a0), (_, l1, a1) = state
            o = a0 / l0 - lam * (a1 / l1)
            r = _rms(o, sg_ref[...], SUBLN_EPS, axis=0) * (1.0 - li)
            o_ref[...] = r.T.astype(BF16)


def _attn_prompt(qt, kb, vt, lam_params, sg_col, li, batch, seq):
    tq = TQ_ATTN
    nq = seq // tq
    q_blk = pl.BlockSpec((None, None, LANES, tq), lambda b, h, i: (h, b * nq + i, 0, 0))
    k_blk = pl.BlockSpec((seq, LANES), lambda b, h, i: (b, h))
    v_blk = pl.BlockSpec((None, nq, LANES, tq), lambda b, h, i: (h, b, 0, 0))
    return pl.pallas_call(
        functools.partial(_attn_prompt_body, tq=tq, nq=nq, li=li),
        grid=(batch, H_A, nq),
        in_specs=[q_blk, k_blk, v_blk] + [_full((1, HD_A))] * 4 + [_full((2 * HD_A, 1))],
        out_specs=pl.BlockSpec((tq, LANES), lambda b, h, i: (b * nq + i, h)),
        out_shape=jax.ShapeDtypeStruct((batch * seq, W_A), BF16),
        compiler_params=_cparams(("parallel", "parallel", "arbitrary")),
        name="attn_prompt",
    )(qt, kb, vt, *lam_params, sg_col)


def _attn_sample_body(*refs, **params):
    n_chunks = pl.num_programs(1)
    t = pl.program_id(0) * n_chunks + pl.program_id(1)
    _sample_attn_step(t, n_chunks, pl.num_programs(0) * n_chunks, *refs, **params)


def _sample_attn_step(t, n_chunks, n_steps, pt_ref, q_ref, kn_ref, vn_ref, ck_hbm, cv_hbm, lq1, lk1, lq2, lk2,
                      sg_ref, o_ref, kbuf, vbuf, sem, m_sc, l_sc, a_sc, *, layer, n_pages_step, n_tok, li,
                      between=None):
    c = lax.rem(t, n_chunks)
    nq = n_tok * H_A
    ahead = PAGE_BUFFERS - 1

    def page_copies(step, slot):
        seq = lax.div(step, n_chunks)
        first = lax.rem(step, n_chunks) * n_pages_step
        out = []
        for j in range(n_pages_step):
            page = pt_ref[seq, first + j]
            out.append(pltpu.make_async_copy(ck_hbm.at[layer, page], kbuf.at[slot, j], sem.at[slot, 0]))
            out.append(pltpu.make_async_copy(cv_hbm.at[layer, page], vbuf.at[slot, j], sem.at[slot, 1]))
        return out

    def fetch(step):
        for cp in page_copies(step, lax.rem(step, PAGE_BUFFERS)):
            cp.start()

    @pl.when(t == 0)
    def _():
        for d in range(ahead):
            @pl.when(d < n_steps)
            def _():
                fetch(jnp.int32(d))

    @pl.when(t + ahead < n_steps)
    def _():
        fetch(t + ahead)

    if between is not None:
        between()

    slot = lax.rem(t, PAGE_BUFFERS)
    pltpu.make_async_copy(ck_hbm.at[layer, pl.ds(0, n_pages_step)], kbuf.at[slot], sem.at[slot, 0]).wait()
    pltpu.make_async_copy(cv_hbm.at[layer, pl.ds(0, n_pages_step)], vbuf.at[slot], sem.at[slot, 1]).wait()

    @pl.when(c == 0)
    def _():
        m_sc[...] = jnp.full(m_sc.shape, -jnp.inf, F32)
        l_sc[...] = jnp.zeros(l_sc.shape, F32)
        a_sc[...] = jnp.zeros(a_sc.shape, F32)

    q1, q2 = _split_maps(q_ref[...], 1)
    qb = jnp.concatenate([q1, q2], axis=0)

    def accumulate(blocks):
        ss = []
        for k, _, mask in blocks:
            s = lax.dot_general(qb, k.astype(BF16), (((1,), (1,)), ((), ())), preferred_element_type=F32)
            ss.append(jnp.where(mask, s, -jnp.inf))
        m_old = m_sc[...]
        m_new = m_old
        for s in ss:
            m_new = jnp.maximum(m_new, jnp.max(s, axis=-1, keepdims=True))
        alpha = jnp.exp2(m_old - m_new)
        l_new = alpha * l_sc[...]
        a_new = alpha * a_sc[...]
        for s, (_, v, _) in zip(ss, blocks):
            p = jnp.exp2(s - m_new)
            l_new = l_new + jnp.sum(p, axis=-1, keepdims=True)
            a_new = a_new + jnp.dot(p.astype(BF16), v.astype(BF16), preferred_element_type=F32)
        m_sc[...] = m_new
        l_sc[...] = l_new
        a_sc[...] = a_new

    n_rows = PAGE_SIZE * H_A
    r = lax.broadcasted_iota(jnp.int32, (2 * nq, n_rows), 0)
    col = lax.broadcasted_iota(jnp.int32, (2 * nq, n_rows), 1)
    head_of = lambda t: t & (H_A - 1)
    head_mask = head_of(r) == head_of(col)
    accumulate([(kbuf[slot, j], vbuf[slot, j], head_mask) for j in range(n_pages_step)])

    @pl.when(c == n_chunks - 1)
    def _():
        rn = lax.broadcasted_iota(jnp.int32, (2 * nq, nq), 0)
        cn = lax.broadcasted_iota(jnp.int32, (2 * nq, nq), 1)
        q_row = jnp.where(rn >= nq, rn - nq, rn)
        new_mask = (head_of(rn) == head_of(cn)) & (cn <= q_row)
        accumulate([(kn_ref[...], vn_ref[...], new_mask)])
        lam = _diff_lambda(lq1, lk1, lq2, lk2, li)
        o = a_sc[...] / l_sc[...]
        o_ref[...] = _rms(o[:nq] - lam * o[nq:], sg_ref[...], SUBLN_EPS) * (1.0 - li)


def _attn_sample(q, k_new, v_new, cache_k, cache_v, page_table, layer, lam_params, sg, li, n_tok):
    n_seq, n_pages = page_table.shape
    pps = math.gcd(PAGES_PER_STEP, n_pages)
    nq = n_tok * H_A
    n_rows = PAGE_SIZE * H_A

    seq_blk = pl.BlockSpec((None, nq, LANES), lambda s, c, pt: (s, 0, 0))
    const = lambda shape: pl.BlockSpec(shape, lambda s, c, pt: (0,) * len(shape))
    page_buf = pltpu.VMEM((PAGE_BUFFERS, pps, n_rows, LANES), F32)
    grid_spec = pltpu.PrefetchScalarGridSpec(
        num_scalar_prefetch=1,
        grid=(n_seq, n_pages // pps),
        in_specs=[seq_blk, seq_blk, seq_blk, _ANY, _ANY] + [const((1, HD_A))] * 4 + [const((1, 2 * HD_A))],
        out_specs=seq_blk,
        scratch_shapes=[page_buf, page_buf, pltpu.SemaphoreType.DMA((PAGE_BUFFERS, 2)),
                        pltpu.VMEM((2 * nq, 1), F32), pltpu.VMEM((2 * nq, 1), F32), pltpu.VMEM((2 * nq, LANES), F32)],
    )
    return pl.pallas_call(
        functools.partial(_attn_sample_body, layer=layer, n_pages_step=pps, n_tok=n_tok, li=li),
        grid_spec=grid_spec,
        out_shape=jax.ShapeDtypeStruct((n_seq, nq, LANES), F32),
        compiler_params=_cparams(("arbitrary", "arbitrary")),
        name="attn_sample",
    )(page_table, q, k_new, v_new, cache_k, cache_v, *lam_params, sg)


def _attn_fused_body(pt_ref, qt_ref, k_ref, vt_ref, sgc_ref, op_ref, q_ref, kn_ref, vn_ref, ck_hbm, cv_hbm,
                     lq1, lk1, lq2, lk2, sgr_ref, os_ref, kbuf, vbuf, sem, m_sc, l_sc, a_sc,
                     *, n_chunks, tq, nq, li, **sample_params):
    i = pl.program_id(2)
    t = (pl.program_id(0) * pl.num_programs(1) + pl.program_id(1)) * nq + i
    n_steps = pl.num_programs(0) * pl.num_programs(1) * nq
    lam_refs = (lq1, lk1, lq2, lk2)

    def prompt_step():
        _prompt_attn_step(i, qt_ref, k_ref, vt_ref, *lam_refs, sgc_ref, op_ref, tq=tq, nq=nq, li=li)

    _sample_attn_step(t, n_chunks, n_steps, pt_ref, q_ref, kn_ref, vn_ref, ck_hbm, cv_hbm, *lam_refs, sgr_ref, os_ref,
                      kbuf, vbuf, sem, m_sc, l_sc, a_sc, li=li, between=prompt_step, **sample_params)


def _attn_fused(qt, kb, vt, q, k_new, v_new, cache_k, cache_v, page_table, layer, lam_params, sg, sg_col, li,
                batch, seq, n_tok, pps):
    n_seq, n_pages = page_table.shape
    tq = TQ_ATTN
    nq_blocks = seq // tq
    n_chunks = n_pages // pps
    assert batch * H_A * nq_blocks == n_seq * n_chunks
    nq = n_tok * H_A
    n_rows = PAGE_SIZE * H_A
    step_of = lambda b, h, i: (b * H_A + h) * nq_blocks + i
    q_blk = pl.BlockSpec((None, None, LANES, tq), lambda b, h, i, pt: (h, b * nq_blocks + i, 0, 0))
    k_blk = pl.BlockSpec((seq, LANES), lambda b, h, i, pt: (b, h))
    v_blk = pl.BlockSpec((None, nq_blocks, LANES, tq), lambda b, h, i, pt: (h, b, 0, 0))
    o_blk = pl.BlockSpec((tq, LANES), lambda b, h, i, pt: (b * nq_blocks + i, h))
    seq_blk = pl.BlockSpec((None, nq, LANES), lambda b, h, i, pt: (step_of(b, h, i) // n_chunks, 0, 0))
    const = lambda shape: pl.BlockSpec(shape, lambda b, h, i, pt: (0,) * len(shape))
    page_buf = pltpu.VMEM((PAGE_BUFFERS, pps, n_rows, LANES), F32)
    grid_spec = pltpu.PrefetchScalarGridSpec(
        num_scalar_prefetch=1,
        grid=(batch, H_A, nq_blocks),
        in_specs=[q_blk, k_blk, v_blk, const((2 * HD_A, 1))]
        + [seq_blk, seq_blk, seq_blk, _ANY, _ANY] + [const((1, HD_A))] * 4 + [const((1, 2 * HD_A))],
        out_specs=[o_blk, seq_blk],
        scratch_shapes=[page_buf, page_buf, pltpu.SemaphoreType.DMA((PAGE_BUFFERS, 2)),
                        pltpu.VMEM((2 * nq, 1), F32), pltpu.VMEM((2 * nq, 1), F32), pltpu.VMEM((2 * nq, LANES), F32)],
    )

    def body(pt_ref, qt_ref, k_ref, vt_ref, sgc_ref, q_ref, kn_ref, vn_ref, ck_hbm, cv_hbm, lq1, lk1, lq2, lk2,
             sgr_ref, op_ref, os_ref, *scratch):
        _attn_fused_body(pt_ref, qt_ref, k_ref, vt_ref, sgc_ref, op_ref, q_ref, kn_ref, vn_ref, ck_hbm, cv_hbm,
                         lq1, lk1, lq2, lk2, sgr_ref, os_ref, *scratch,
                         n_chunks=n_chunks, tq=tq, nq=nq_blocks, li=li, layer=layer, n_pages_step=pps, n_tok=n_tok)

    return pl.pallas_call(
        body,
        grid_spec=grid_spec,
        out_shape=[jax.ShapeDtypeStruct((batch * seq, W_A), BF16), jax.ShapeDtypeStruct((n_seq, nq, LANES), F32)],
        compiler_params=_cparams(("arbitrary", "arbitrary", "arbitrary")),
        name="attn_fused",
    )(page_table, qt, kb, vt, sg_col, q, k_new, v_new, cache_k, cache_v, *lam_params, sg)


def _group_select(vals):
    lane = lax.broadcasted_iota(jnp.int32, vals[0].shape, 1)
    out = vals[-1]
    for g in range(len(vals) - 2, -1, -1):
        out = jnp.where(lane < (g + 1) * CG_B, vals[g], out)
    return out


def _router(xn, wr_hi, wr_lo, br):
    hi = xn.astype(BF16)
    lo = (xn - hi.astype(F32)).astype(BF16)
    logits = (jnp.dot(hi, wr_hi[...], preferred_element_type=F32)
              + jnp.dot(lo, wr_hi[...], preferred_element_type=F32)
              + jnp.dot(hi, wr_lo[...], preferred_element_type=F32)) + br[...]
    lane = lax.broadcasted_iota(jnp.int32, logits.shape, 1).astype(F32)
    neg = -jnp.inf
    big = float(LANES)

    def first_argmax(vals):
        top = jnp.max(vals, axis=-1, keepdims=True)
        idx = jnp.min(jnp.where(vals == top, lane, big), axis=-1, keepdims=True)
        return top, idx

    is_group = lane < N_GROUPS
    g_top, g_idx = first_argmax(jnp.where(is_group, logits, neg))
    g_w = 1.0 / jnp.sum(jnp.where(is_group, jnp.exp(logits - g_top), 0.0), axis=-1, keepdims=True)
    e_base = N_GROUPS + EXP_PER_GROUP * g_idx
    in_group = (lane >= e_base) & (lane < e_base + EXP_PER_GROUP)
    e_vals = jnp.where(in_group, logits, neg)
    e1, i1 = first_argmax(e_vals)
    e2, i2 = first_argmax(jnp.where(lane == i1, neg, e_vals))
    t = jnp.exp(e2 - e1)
    w1 = g_w / (1.0 + t)
    w2 = g_w * t / (1.0 + t)
    a1 = i1 - e_base
    a2 = i2 - e_base
    lo_id = jnp.minimum(a1, a2)
    hi_id = jnp.maximum(a1, a2)
    w_lo = jnp.where(a1 < a2, w1, w2)
    w_hi = jnp.where(a1 < a2, w2, w1)
    pair_base = jnp.where(lo_id == 0.0, 0.0, jnp.where(lo_id == 1.0, 3.0, 5.0))
    bucket = g_idx * N_PAIRS + pair_base + (hi_id - lo_id - 1.0)
    return jnp.where(lane == 0.0, bucket, jnp.where(lane == 1.0, w_lo, jnp.where(lane == 2.0, w_hi, 0.0)))


def _mix_tail(a, b_out, c_out, x, wout_ref, gffn_ref, wr_hi, wr_lo, br, y_ref, xg_ref, rt_ref, cnt_ref, *, exact):
    if exact:
        mm = lambda t, lo, hi: jnp.dot(t, wout_ref[lo:hi, :], preferred_element_type=F32, precision=HIGHEST)
    else:
        mm = lambda t, lo, hi: jnp.dot(t.astype(BF16), wout_ref[lo:hi, :], preferred_element_type=F32)
    y = x + (mm(a, 0, W_A) + mm(b_out, W_A, W_A + W_B) + mm(c_out, W_A + W_B, D_MODEL))
    y_ref[...] = y
    xn = _rms(y, gffn_ref[...], NORM_EPS)
    route = _router(xn, wr_hi, wr_lo, br)
    xg_ref[:, 0:D_MODEL] = xn
    xg_ref[:, D_MODEL:] = route
    rt_ref[...] = route.T[0:ROUTE_FIELDS, :]
    lane = lax.broadcasted_iota(jnp.int32, route.shape, 1).astype(F32)
    hits = jnp.sum(jnp.where(lane == route[:, 0:1], 1.0, 0.0), axis=0, keepdims=True)

    @pl.when(pl.program_id(0) == 0)
    def _():
        cnt_ref[...] = jnp.zeros(cnt_ref.shape, F32)

    cnt_ref[...] += hits


def _sgu_norm(gv, sgu_g, sgu_b):
    gg = _gelu(gv)
    mu = jnp.mean(gg, axis=-1, keepdims=True)
    d = gg - mu
    var = jnp.mean(d * d, axis=-1, keepdims=True)
    return d * lax.rsqrt(var + NORM_EPS) * sgu_g + sgu_b


def _mix_prompt_body(a_ref, p_ref, halo_ref, u_ref, gv_ref, x_ref, wpool_ref, pscale_ref, sgug_ref, sgub_ref,
                     ws_ref, bs_ref, wout_ref, gffn_ref, wr_hi, wr_lo, br,
                     y_ref, xg_ref, rt_ref, cnt_ref, ext_sc, *, tm, tiles_per_seq):
    i = pl.program_id(0)
    t_in_seq = i % tiles_per_seq
    p = p_ref[...]
    ext_sc[0:POOL_HALO, :] = jnp.where(t_in_seq == 0, 0.0, halo_ref[...])
    ext_sc[POOL_HALO:, :] = p
    pos = t_in_seq * tm + lax.broadcasted_iota(jnp.int32, (tm, 1), 0)
    acc = p
    means = []
    for k in range(1, max(POOL_WINDOWS)):
        acc = acc + ext_sc[POOL_HALO - k:POOL_HALO - k + tm, :]
        if k + 1 in POOL_WINDOWS:
            cnt = jnp.minimum(pos + 1, k + 1).astype(F32)
            means.append(acc / cnt)
    d = (_group_select(means) - p).astype(BF16)
    b_out = jnp.dot(d, wpool_ref[...], preferred_element_type=F32) * pscale_ref[...]

    vn = _sgu_norm(gv_ref[...], sgug_ref[...], sgub_ref[...]).astype(BF16)
    r = lax.broadcasted_iota(jnp.int32, (CHUNK, CHUNK), 0)
    c = lax.broadcasted_iota(jnp.int32, (CHUNK, CHUNK), 1)
    w_tril = [jnp.where(r >= c, ws_ref[h], 0.0).astype(BF16) for h in range(H_C)]
    chunks = []
    for ci in range(tm // CHUNK):
        vc = vn[ci * CHUNK:(ci + 1) * CHUNK, :]
        per_head = [jnp.dot(w_tril[h], vc, preferred_element_type=F32) for h in range(H_C)]
        chunks.append(_group_select(per_head) + bs_ref[...])
    mixed = jnp.concatenate(chunks, axis=0)
    c_out = _gelu(u_ref[...]) * mixed

    _mix_tail(a_ref[...], b_out, c_out, x_ref[...], wout_ref, gffn_ref, wr_hi, wr_lo, br,
              y_ref, xg_ref, rt_ref, cnt_ref, exact=False)


def _mix_out_specs(n, tm):
    row = lambda w: pl.BlockSpec((tm, w), lambda i: (i, 0))
    specs = [row(D_MODEL), row(D_ROUTED), pl.BlockSpec((ROUTE_FIELDS, tm), lambda i: (0, i)), _full((1, LANES))]
    shapes = [jax.ShapeDtypeStruct((n, D_MODEL), F32), jax.ShapeDtypeStruct((n, D_ROUTED), F32),
              jax.ShapeDtypeStruct((ROUTE_FIELDS, n), F32), jax.ShapeDtypeStruct((1, LANES), F32)]
    return specs, shapes


def _mix_prompt(a, p, u, gv, x, lw, seq):
    n = x.shape[0]
    tm = TM_MIX
    tiles_per_seq = seq // tm
    halo_per_tile = tm // POOL_HALO
    row = lambda w: pl.BlockSpec((tm, w), lambda i: (i, 0))
    halo = pl.BlockSpec((POOL_HALO, W_B), lambda i: (jnp.maximum(i * halo_per_tile - 1, 0), 0))
    out_specs, out_shape = _mix_out_specs(n, tm)
    return pl.pallas_call(
        functools.partial(_mix_prompt_body, tm=tm, tiles_per_seq=tiles_per_seq),
        grid=(n // tm,),
        in_specs=[row(W_A), row(W_B), halo, row(W_C), row(W_C), row(D_MODEL),
                  _full((W_B, W_B)), _full((1, W_B)), _full((1, W_C)), _full((1, W_C)),
                  _full((H_C, CHUNK, CHUNK)), _full((CHUNK, W_C)), _full((D_MODEL, D_MODEL)),
                  _full((1, D_MODEL)), _full((D_MODEL, LANES)), _full((D_MODEL, LANES)), _full((1, LANES))],
        out_specs=out_specs,
        out_shape=out_shape,
        scratch_shapes=[pltpu.VMEM((POOL_HALO + tm, W_B), F32)],
        compiler_params=_cparams(("arbitrary",)),
        name="mix_prompt",
    )(a, p, p, u, gv, x, lw["wpool"], lw["pscale"], lw["sgu_g"], lw["sgu_b"], lw["w_s"], lw["bs_chunk"],
      lw["w_out"], lw["g_ffn"], lw["wr_hi"], lw["wr_lo"], lw["br"])


def _mix_sample_body(a_ref, pext_ref, u_ref, gv_ref, x_ref, wpool_ref, pscale_ref, sgug_ref, sgub_ref,
                     wl_ref, bl_ref, wout_ref, gffn_ref, wr_hi, wr_lo, br,
                     y_ref, xg_ref, rt_ref, cnt_ref, vn_ref, *, n_tok, n_seq):
    ds = []
    for i in range(n_tok):
        cur = pext_ref[POOL_BUF + i]
        acc = cur
        means = []
        for k in range(1, max(POOL_WINDOWS)):
            acc = acc + pext_ref[POOL_BUF + i - k]
            if k + 1 in POOL_WINDOWS:
                means.append(acc / float(k + 1))
        ds.append(_group_select(means) - cur)
    d = jnp.concatenate(ds, axis=0)
    b_out = jnp.dot(d, wpool_ref[...], preferred_element_type=F32, precision=HIGHEST) * pscale_ref[...]

    vn = _sgu_norm(gv_ref[...], sgug_ref[...], sgub_ref[...])
    vn_ref[...] = vn
    rows = []
    for i in range(n_tok):
        acc = jnp.broadcast_to(bl_ref[i:i + 1, :], (n_seq, W_C))
        for j in range(i + 1):
            acc = acc + wl_ref[i * n_tok + j:i * n_tok + j + 1, :] * vn[j * n_seq:(j + 1) * n_seq, :]
        rows.append(acc)
    c_out = _gelu(u_ref[...]) * jnp.concatenate(rows, axis=0)

    _mix_tail(a_ref[...], b_out, c_out, x_ref[...], wout_ref, gffn_ref, wr_hi, wr_lo, br,
              y_ref, xg_ref, rt_ref, cnt_ref, exact=True)


def _mix_sample(a, pext, u, gv, x, lw, n_tok, n_seq):
    n = x.shape[0]
    out_specs, out_shape = _mix_out_specs(n, n)
    return pl.pallas_call(
        functools.partial(_mix_sample_body, n_tok=n_tok, n_seq=n_seq),
        grid=(1,),
        in_specs=[_full((n, W_A)), _full((POOL_BUF + n_tok, n_seq, W_B)), _full((n, W_C)), _full((n, W_C)),
                  _full((n, D_MODEL)), _full((W_B, W_B)), _full((1, W_B)), _full((1, W_C)), _full((1, W_C)),
                  _full((n_tok * n_tok, W_C)), _full((n_tok, W_C)), _full((D_MODEL, D_MODEL)),
                  _full((1, D_MODEL)), _full((D_MODEL, LANES)), _full((D_MODEL, LANES)), _full((1, LANES))],
        out_specs=out_specs + [_full((n, W_C))],
        out_shape=out_shape + [jax.ShapeDtypeStruct((n, W_C), F32)],
        compiler_params=_cparams(("arbitrary",)),
        name="mix_sample",
    )(a, pext, u, gv, x, lw["wpool_f32"], lw["pscale"], lw["sgu_g"], lw["sgu_b"], lw["wl_new"], lw["bl_new"],
      lw["w_out_f32"], lw["g_ffn"], lw["wr_hi"], lw["wr_lo"], lw["br"])


def _moe_body(src_ref, elo_ref, ehi_ref, nv_ref, nact_ref, x_hbm, wg_lo, wg_hi, wu_lo, wu_hi, wd_lo, wd_hi,
              y_hbm, xbuf, ybuf, gsem, ssem, *, tm):
    i = pl.program_id(0)
    n_act = nact_ref[0]
    slot = lax.rem(i, 2)

    def row_copy_in(tile, s, r):
        return pltpu.make_async_copy(x_hbm.at[pl.ds(src_ref[tile * tm + r], 1)], xbuf.at[s, pl.ds(r, 1)], gsem.at[s])

    def row_copy_out(tile, s, r):
        return pltpu.make_async_copy(ybuf.at[s, pl.ds(r, 1)], y_hbm.at[pl.ds(src_ref[tile * tm + r], 1)], ssem.at[s])

    def start_gather(tile, s, unrolled):
        if unrolled:
            for r in range(tm):
                row_copy_in(tile, s, r).start(priority=r % 2)
            return

        def one(r, c):
            row_copy_in(tile, s, r).start()
            return c
        lax.fori_loop(0, tm, one, 0, unroll=8)

    def wait_gather(s):
        pltpu.make_async_copy(x_hbm.at[pl.ds(0, tm)], xbuf.at[s], gsem.at[s]).wait()

    def whole_groups(n):
        return pl.multiple_of(lax.shift_left(lax.shift_right_logical(n, 3), 3), SUBLANES)

    def start_scatter(tile, s):
        n = nv_ref[tile]

        @pl.when(n == tm)
        def _():
            for r in range(tm):
                row_copy_out(tile, s, r).start(priority=r % 2)

        @pl.when(n < tm)
        def _():
            def group(g, c):
                for u in range(SUBLANES):
                    row_copy_out(tile, s, g * SUBLANES + u).start()
                return c
            lax.fori_loop(0, lax.shift_right_logical(n, 3), group, 0)

            def one(r, c):
                row_copy_out(tile, s, r).start()
                return c
            lax.fori_loop(whole_groups(n), n, one, 0)

    def wait_scatter(tile, s):
        n = nv_ref[tile]
        n_blk = whole_groups(n)

        @pl.when(n_blk > 0)
        def _():
            pltpu.make_async_copy(ybuf.at[s, pl.ds(0, n_blk)], y_hbm.at[pl.ds(0, n_blk)], ssem.at[s]).wait()

        def one(r, c):
            pltpu.make_async_copy(ybuf.at[s, pl.ds(0, 1)], y_hbm.at[pl.ds(0, 1)], ssem.at[s]).wait()
            return c
        lax.fori_loop(n_blk, n, one, 0)

    @pl.when(i == 0)
    def _():
        start_gather(0, 0, False)

    @pl.when(i + 1 < n_act)
    def _():
        start_gather(i + 1, 1 - slot, True)

    @pl.when(i < n_act)
    def _():
        wait_gather(slot)

        @pl.when(i >= 2)
        def _():
            wait_scatter(i - 2, slot)

        xrow = xbuf[slot]
        x = xrow[:, 0:D_MODEL].astype(BF16)
        gate_lo = xrow[:, D_MODEL + 1:D_MODEL + 2]
        gate_hi = xrow[:, D_MODEL + 2:D_MODEL + 3]

        def hidden(wg, wu, gate):
            hg = jnp.dot(x, wg[...], preferred_element_type=F32)
            hu = jnp.dot(x, wu[...], preferred_element_type=F32)
            return (hg / (1.0 + jnp.exp(-hg)) * hu * gate).astype(BF16)

        h_lo = hidden(wg_lo, wu_lo, gate_lo)
        h_hi = hidden(wg_hi, wu_hi, gate_hi)
        ybuf[slot] = (jnp.dot(h_lo, wd_lo[...], preferred_element_type=F32)
                      + jnp.dot(h_hi, wd_hi[...], preferred_element_type=F32))
        start_scatter(i, slot)

        @pl.when(i == n_act - 1)
        def _():
            wait_scatter(i, slot)

            @pl.when(i >= 1)
            def _():
                wait_scatter(i - 1, 1 - slot)


def _moe(xg, src, e_lo, e_hi, n_valid, n_active, wg, wu, wd, tm):
    n = xg.shape[0]
    n_tiles = e_lo.shape[0]
    wspec = lambda shape, which: pl.BlockSpec(
        (None,) + shape, lambda i, src, elo, ehi, nv, na: ((elo, ehi)[which][i], 0, 0))
    up, dn = (D_MODEL, D_FF), (D_FF, D_MODEL)
    grid_spec = pltpu.PrefetchScalarGridSpec(
        num_scalar_prefetch=5,
        grid=(n_tiles,),
        in_specs=[_ANY, wspec(up, 0), wspec(up, 1), wspec(up, 0), wspec(up, 1), wspec(dn, 0), wspec(dn, 1)],
        out_specs=_ANY,
        scratch_shapes=[pltpu.VMEM((2, tm, D_ROUTED), F32), pltpu.VMEM((2, tm, D_MODEL), F32),
                        pltpu.SemaphoreType.DMA((2,)), pltpu.SemaphoreType.DMA((2,))],
    )
    return pl.pallas_call(
        functools.partial(_moe_body, tm=tm),
        grid_spec=grid_spec,
        out_shape=jax.ShapeDtypeStruct((n, D_MODEL), F32),
        compiler_params=_cparams(("arbitrary",)),
        name="moe",
    )(src, e_lo, e_hi, n_valid, n_active, xg, wg, wg, wu, wu, wd, wd)


_PAIR_LO = (0, 0, 0, 1, 1, 2)
_PAIR_HI = (1, 2, 3, 2, 3, 3)


def _moe_dispatch(xg, route_t, cnt, wg, wu, wd, tm):
    n = xg.shape[0]
    n_tiles = -(-n // tm) + N_BUCKETS
    bucket = route_t[0].astype(jnp.int32)
    order = jnp.argsort(bucket, stable=True).astype(jnp.int32)
    counts = cnt[0, :N_BUCKETS].astype(jnp.int32)
    padded = ((counts + tm - 1) // tm) * tm
    ends = jnp.cumsum(padded)
    starts = ends - padded
    sorted_starts = jnp.cumsum(counts) - counts
    tile_start = jnp.arange(n_tiles, dtype=jnp.int32) * tm
    tile_bucket = jnp.minimum(jnp.sum(tile_start[:, None] >= ends[None, :], axis=1, dtype=jnp.int32), N_BUCKETS - 1)
    in_bucket = tile_start - starts[tile_bucket]
    n_valid = jnp.where(tile_start < ends[-1], jnp.clip(counts[tile_bucket] - in_bucket, 0, tm), 0).astype(jnp.int32)
    r_in = jnp.arange(tm, dtype=jnp.int32)[None, :]
    sorted_idx = (sorted_starts[tile_bucket] + in_bucket)[:, None] + r_in
    src = jnp.where(r_in < n_valid[:, None], order[jnp.clip(sorted_idx, 0, n - 1)], 0).reshape(-1)
    group = tile_bucket // N_PAIRS
    pair = tile_bucket % N_PAIRS
    e_lo = group * EXP_PER_GROUP + jnp.asarray(_PAIR_LO, jnp.int32)[pair]
    e_hi = group * EXP_PER_GROUP + jnp.asarray(_PAIR_HI, jnp.int32)[pair]
    n_active = (ends[-1] // tm).reshape(1).astype(jnp.int32)
    return _moe(xg, src.astype(jnp.int32), e_lo, e_hi, n_valid, n_active, wg, wu, wd, tm)


def _final_body(x_ref, y_ref, g_ref, o_ref):
    o_ref[...] = _rms(x_ref[...] + y_ref[...], g_ref[...], NORM_EPS)


def _final_norm(x, y_moe, g, tm):
    n = x.shape[0]
    row = pl.BlockSpec((tm, D_MODEL), lambda i: (i, 0))
    return pl.pallas_call(
        _final_body,
        grid=(n // tm,),
        in_specs=[row, row, _full((1, D_MODEL))],
        out_specs=row,
        out_shape=jax.ShapeDtypeStruct((n, D_MODEL), F32),
        compiler_params=_cparams(("parallel",)),
        name="final_norm",
    )(x, y_moe, g.reshape(1, D_MODEL))


def _layer_weights(l, n_tok, norm_ffn_g, w_out, subln_g, w_pool, pool_scale, sgu_norm_g, sgu_norm_b,
                   w_spatial, b_spatial, w_router_group, b_router_group, w_router_expert, b_router_expert):
    wpool = jnp.zeros((W_B, W_B), F32)
    for g in range(len(POOL_WINDOWS)):
        wpool = wpool.at[g * CG_B:(g + 1) * CG_B, g * CG_B:(g + 1) * CG_B].set(w_pool[l, g])
    wr = jnp.concatenate([w_router_group[l], w_router_expert[l],
                          jnp.zeros((D_MODEL, LANES - N_GROUPS - N_EXPERTS), F32)], axis=1)
    wr_hi = wr.astype(BF16)
    br = jnp.concatenate([b_router_group[l], b_router_expert[l],
                          jnp.zeros((LANES - N_GROUPS - N_EXPERTS,), F32)]).reshape(1, LANES)
    per_lane = lambda t: jnp.repeat(t, HD_C, axis=-1)
    ws_new = jnp.tril(w_spatial[l, :, :n_tok, :n_tok])
    return dict(
        wpool=wpool.astype(BF16), wpool_f32=wpool, pscale=pool_scale[l].reshape(1, W_B),
        sgu_g=sgu_norm_g[l].reshape(1, W_C), sgu_b=sgu_norm_b[l].reshape(1, W_C),
        w_s=w_spatial[l], bs_chunk=per_lane(b_spatial[l].T),
        wl_new=per_lane(jnp.transpose(ws_new, (1, 2, 0)).reshape(n_tok * n_tok, H_C)),
        bl_new=per_lane(b_spatial[l, :, :n_tok].T),
        w_out=w_out[l].astype(BF16), w_out_f32=w_out[l], g_ffn=norm_ffn_g[l].reshape(1, D_MODEL),
        wr_hi=wr_hi, wr_lo=(wr - wr_hi.astype(F32)).astype(BF16), br=br,
        sg=subln_g[l].reshape(1, 2 * HD_A), sg_col=subln_g[l].reshape(2 * HD_A, 1),
    )


def kernel(x_prompt, x_sample, cache_k, cache_v, state_pool, page_table, norm_mix_g, norm_ffn_g, w_in, w_out, lambda_q1, lambda_k1, lambda_q2, lambda_k2, subln_g, w_pool, pool_scale, sgu_norm_g, sgu_norm_b, w_spatial, b_spatial, w_router_group, b_router_group, w_router_expert, b_router_expert, w_exp_gate, w_exp_up, w_exp_down, final_norm_g):
    batch, seq, _ = x_prompt.shape
    n_seq, n_tok, _ = x_sample.shape
    depth = w_in.shape[0]
    n_p = batch * seq
    n_s = n_seq * n_tok
    assert seq % TM_MIX == 0 and seq % TQ_ATTN == 0 and n_p % TM_INPROJ == 0 and TM_INPROJ % TQ_ATTN == 0
    assert n_s % SUBLANES == 0 and page_table.shape[1] * PAGE_SIZE >= max(POOL_WINDOWS)

    xp = x_prompt.reshape(n_p, D_MODEL)
    xs = jnp.transpose(x_sample, (1, 0, 2)).reshape(n_s, D_MODEL)
    yp_moe = ys_moe = None
    kv_prompt = tuple(jnp.zeros((depth, n_p * H_A, 2 * HD_A), F32) for _ in range(2))
    ck = cache_k.reshape(depth, cache_k.shape[1], PAGE_SIZE * H_A, 2 * HD_A)
    cv = cache_v.reshape(depth, cache_v.shape[1], PAGE_SIZE * H_A, 2 * HD_A)

    def to_seq_major(t, width):
        return jnp.transpose(t.reshape(n_tok, n_seq, width), (1, 0, 2))

    pp_l, ks_l, vs_l, ps_l, cs_l = [], [], [], [], []
    for l in range(depth):
        li = _lambda_init(l)
        lw = _layer_weights(l, n_tok, norm_ffn_g, w_out, subln_g, w_pool, pool_scale, sgu_norm_g, sgu_norm_b,
                            w_spatial, b_spatial, w_router_group, b_router_group, w_router_expert, b_router_expert)
        lam_params = tuple(t[l].reshape(1, HD_A) for t in (lambda_q1, lambda_k1, lambda_q2, lambda_k2))
        wg, wu, wd = w_exp_gate[l].astype(BF16), w_exp_up[l].astype(BF16), w_exp_down[l].astype(BF16)

        xp, qt, k_all, v_all, kb, vt, p_p, u_p, gv_p = _inproj_prompt(
            xp, yp_moe, norm_mix_g[l], w_in[l].astype(BF16), l, depth, kv_prompt)
        kv_prompt = (k_all, v_all)
        xs, q, k, v, p, u, gv = _inproj_sample(xs, ys_moe, norm_mix_g[l], w_in[l])
        per_seq = lambda t: to_seq_major(t, W_A).reshape(n_seq, n_tok * H_A, 2 * HD_A)
        n_pages = page_table.shape[1]
        pps = math.gcd(PAGES_PER_STEP, n_pages)
        if batch * H_A * (seq // TQ_ATTN) == n_seq * (n_pages // pps):
            a_p, a = _attn_fused(qt, kb, vt, per_seq(q), per_seq(k), per_seq(v), ck, cv, page_table, l,
                                 lam_params, lw["sg"], lw["sg_col"], li, batch, seq, n_tok, pps)
        else:
            a_p = _attn_prompt(qt, kb, vt, lam_params, lw["sg_col"], li, batch, seq)
            a = _attn_sample(per_seq(q), per_seq(k), per_seq(v), ck, cv, page_table, l, lam_params, lw["sg"], li,
                             n_tok)

        xp, xg, route_t, cnt = _mix_prompt(a_p, p_p, u_p, gv_p, xp, lw, seq)
        yp_moe = _moe_dispatch(xg, route_t, cnt, wg, wu, wd, TM_MOE)
        pp_l.append(p_p.reshape(batch, seq, W_B)[:, seq - POOL_BUF:])

        a = jnp.transpose(a.reshape(n_seq, n_tok, W_A), (1, 0, 2)).reshape(n_s, W_A)
        pext = jnp.concatenate([jnp.transpose(state_pool[l], (1, 0, 2)), p.reshape(n_tok, n_seq, W_B)], axis=0)
        xs, xg, route_t, cnt, vn = _mix_sample(a, pext, u, gv, xs, lw, n_tok, n_seq)
        ys_moe = _moe_dispatch(xg, route_t, cnt, wg, wu, wd, TM_MOE_SAMPLE)
        ks_l.append(to_seq_major(k, W_A).reshape(n_seq, n_tok, H_A, 2 * HD_A))
        vs_l.append(to_seq_major(v, W_A).reshape(n_seq, n_tok, H_A, 2 * HD_A))
        ps_l.append(jnp.transpose(pext[n_tok:], (1, 0, 2)))
        cs_l.append(to_seq_major(vn, W_C))

    y_prompt = _final_norm(xp, yp_moe, final_norm_g, TM_FINAL).reshape(batch, seq, D_MODEL)
    y_sample = to_seq_major(_final_norm(xs, ys_moe, final_norm_g, n_s), D_MODEL)
    k_prompt = kv_prompt[0].reshape(depth, batch, seq, H_A, 2 * HD_A)
    v_prompt = kv_prompt[1].reshape(depth, batch, seq, H_A, 2 * HD_A)
    return (y_prompt, y_sample, k_prompt, v_prompt, jnp.stack(pp_l),
            jnp.stack(ks_l), jnp.stack(vs_l), jnp.stack(ps_l), jnp.stack(cs_l))
```

```python
import functools
import math

import jax
import jax.numpy as jnp
from jax import lax
from jax.experimental import pallas as pl
from jax.experimental.pallas import tpu as pltpu

F32 = jnp.float32
BF16 = jnp.bfloat16
HIGHEST = lax.Precision.HIGHEST

D_MODEL = 1024
H_A = 4
HD_A = 64
W_A = H_A * 2 * HD_A
POOL_WINDOWS = (2, 4, 8, 16)
CG_B = 64
W_B = len(POOL_WINDOWS) * CG_B
POOL_BUF = max(POOL_WINDOWS) - 1
H_C = 4
HD_C = 64
W_C = H_C * HD_C
CHUNK = 128
D_IN = 3 * W_A + W_B + 2 * W_C
N_GROUPS = 4
EXP_PER_GROUP = 4
N_EXPERTS = N_GROUPS * EXP_PER_GROUP
N_PAIRS = 6
N_BUCKETS = N_GROUPS * N_PAIRS
D_FF = D_MODEL // 2
PAGE_SIZE = 128
NORM_EPS = 1e-6
SUBLN_EPS = 1e-5

LANES = 128
SUBLANES = 8
POOL_HALO = 16
VMEM_LIMIT = 56 * 1024 * 1024
D_ROUTED = D_MODEL + LANES
ROUTE_FIELDS = SUBLANES

Q_SCALE = (HD_A ** -0.5) * math.log2(math.e)

TM_INPROJ = 512
TQ_ATTN = 512
QC_ATTN = 128
TM_MIX = 512
TM_MOE = 256
TM_MOE_SAMPLE = 16
TM_FINAL = 512
PAGES_PER_STEP = 16
PAGE_BUFFERS = 3


def _lambda_init(layer):
    return 0.8 - 0.6 * math.exp(-0.3 * layer)


def _cparams(sem):
    return pltpu.CompilerParams(dimension_semantics=sem, vmem_limit_bytes=VMEM_LIMIT)


def _full(shape):
    return pl.BlockSpec(shape, lambda *_: (0,) * len(shape))


_ANY = pl.BlockSpec(memory_space=pl.ANY)


def _rms(x, g, eps, axis=-1):
    return x * lax.rsqrt(jnp.mean(x * x, axis=axis, keepdims=True) + eps) * g


def _gelu(x):
    return 0.5 * x * (1.0 + lax.erf(x * (2.0 ** -0.5)))


def _inproj_prompt_body(*refs, has_moe, has_prev, tm, tq):
    refs = list(refs)
    x_ref = refs.pop(0)
    x = x_ref[...]
    if has_moe:
        x = x + refs.pop(0)[...]
    g_ref, w_ref = refs.pop(0), refs.pop(0)
    if has_prev:
        refs.pop(0), refs.pop(0)
    if has_moe:
        refs.pop(0)[...] = x
    qt_ref, k_ref, v_ref, kb_ref, vt_ref, p_ref, u_ref, gv_ref = refs
    xn = _rms(x, g_ref[...], NORM_EPS).astype(BF16)

    def seg(lo, hi):
        return jnp.dot(xn, w_ref[:, lo:hi], preferred_element_type=F32)

    def put_feature_major(ref, t):
        tt = t.T.astype(BF16)
        for h in range(H_A):
            for s in range(tm // tq):
                ref[h, s] = tt[h * LANES:(h + 1) * LANES, s * tq:(s + 1) * tq]

    def put_head_rows(ref, t):
        for h in range(H_A):
            ref[pl.ds(h, tm, stride=H_A), :] = t[:, h * LANES:(h + 1) * LANES]

    put_feature_major(qt_ref, seg(0, W_A) * Q_SCALE)
    k = seg(W_A, 2 * W_A)
    put_head_rows(k_ref, k)
    kb_ref[...] = k.astype(BF16)
    v = seg(2 * W_A, 3 * W_A)
    put_head_rows(v_ref, v)
    put_feature_major(vt_ref, v)
    o = 3 * W_A
    p_ref[...] = seg(o, o + W_B)
    u_ref[...] = seg(o + W_B, o + W_B + W_C)
    gv_ref[...] = seg(o + W_B + W_C, D_IN)


def _inproj_prompt(x, y_moe, g, w_bf, layer, depth, kv_prev):
    n = x.shape[0]
    tm, tq = TM_INPROJ, TQ_ATTN
    has_moe = y_moe is not None
    has_prev = kv_prev is not None
    row = lambda w: pl.BlockSpec((tm, w), lambda i: (i, 0))
    fmajor = pl.BlockSpec((H_A, tm // tq, LANES, tq), lambda i: (0, i, 0, 0))
    kv_rows = pl.BlockSpec((None, tm * H_A, LANES), lambda i: (layer, i, 0))
    in_specs = [row(D_MODEL)] + ([row(D_MODEL)] if has_moe else []) + [_full((1, D_MODEL)), _full((D_MODEL, D_IN))]
    args = [x] + ([y_moe] if has_moe else []) + [g.reshape(1, D_MODEL), w_bf]
    aliases = {}
    n_lead_out = 1 if has_moe else 0
    if has_prev:
        aliases = {len(args): n_lead_out + 1, len(args) + 1: n_lead_out + 2}
        in_specs += [_ANY, _ANY]
        args += list(kv_prev)
    fm_shape = jax.ShapeDtypeStruct((H_A, n // tq, LANES, tq), BF16)
    kv_shape = jax.ShapeDtypeStruct((depth, n * H_A, LANES), F32)
    outs = [
        (fm_shape, fmajor),
        (kv_shape, kv_rows),
        (kv_shape, kv_rows),
        (jax.ShapeDtypeStruct((n, W_A), BF16), row(W_A)),
        (fm_shape, fmajor),
        (jax.ShapeDtypeStruct((n, W_B), F32), row(W_B)),
        (jax.ShapeDtypeStruct((n, W_C), F32), row(W_C)),
        (jax.ShapeDtypeStruct((n, W_C), F32), row(W_C)),
    ]
    if has_moe:
        outs = [(jax.ShapeDtypeStruct((n, D_MODEL), F32), row(D_MODEL))] + outs
    res = pl.pallas_call(
        functools.partial(_inproj_prompt_body, has_moe=has_moe, has_prev=has_prev, tm=tm, tq=tq),
        grid=(n // tm,),
        in_specs=in_specs,
        out_specs=[s for _, s in outs],
        out_shape=[s for s, _ in outs],
        input_output_aliases=aliases,
        compiler_params=_cparams(("parallel",)),
        name="inproj_prompt",
    )(*args)
    if has_moe:
        return res
    return [x] + list(res)


def _inproj_sample_body(*refs, has_moe):
    refs = list(refs)
    x = refs.pop(0)[...]
    if has_moe:
        x = x + refs.pop(0)[...]
    g_ref, w_ref = refs.pop(0), refs.pop(0)
    if has_moe:
        refs.pop(0)[...] = x
    q_ref, k_ref, v_ref, p_ref, u_ref, gv_ref = refs
    xn = _rms(x, g_ref[...], NORM_EPS)

    def seg(lo, hi):
        return jnp.dot(xn, w_ref[:, lo:hi], preferred_element_type=F32, precision=HIGHEST)

    q_ref[...] = (seg(0, W_A) * Q_SCALE).astype(BF16)
    k_ref[...] = seg(W_A, 2 * W_A)
    v_ref[...] = seg(2 * W_A, 3 * W_A)
    o = 3 * W_A
    p_ref[...] = seg(o, o + W_B)
    u_ref[...] = seg(o + W_B, o + W_B + W_C)
    gv_ref[...] = seg(o + W_B + W_C, D_IN)


def _inproj_sample(x, y_moe, g, w_f32):
    n = x.shape[0]
    has_moe = y_moe is not None
    blk = lambda w: _full((n, w))
    outs = [(jax.ShapeDtypeStruct((n, W_A), BF16), blk(W_A)), (jax.ShapeDtypeStruct((n, W_A), F32), blk(W_A)),
            (jax.ShapeDtypeStruct((n, W_A), F32), blk(W_A)), (jax.ShapeDtypeStruct((n, W_B), F32), blk(W_B)),
            (jax.ShapeDtypeStruct((n, W_C), F32), blk(W_C)), (jax.ShapeDtypeStruct((n, W_C), F32), blk(W_C))]
    if has_moe:
        outs = [(jax.ShapeDtypeStruct((n, D_MODEL), F32), blk(D_MODEL))] + outs
    args = [x] + ([y_moe] if has_moe else []) + [g.reshape(1, D_MODEL), w_f32]
    res = pl.pallas_call(
        functools.partial(_inproj_sample_body, has_moe=has_moe),
        grid=(1,),
        in_specs=[blk(D_MODEL)] * (2 if has_moe else 1) + [_full((1, D_MODEL)), _full((D_MODEL, D_IN))],
        out_specs=[s for _, s in outs],
        out_shape=[s for s, _ in outs],
        compiler_params=_cparams(("arbitrary",)),
        name="inproj_sample",
    )(*args)
    if has_moe:
        return res
    return [x] + list(res)


def _diff_lambda(lq1, lk1, lq2, lk2, li):
    s1 = jnp.sum(lq1[...] * lk1[...], axis=-1, keepdims=True)
    s2 = jnp.sum(lq2[...] * lk2[...], axis=-1, keepdims=True)
    return jnp.exp(s1) - jnp.exp(s2) + li


def _split_maps(q, axis):
    idx = lax.broadcasted_iota(jnp.int32, q.shape, axis)
    zero = jnp.zeros_like(q)
    return jnp.where(idx < HD_A, q, zero), jnp.where(idx >= HD_A, q, zero)


def _attn_prompt_body(*refs, tq, nq, li):
    _prompt_attn_step(pl.program_id(2), *refs, tq=tq, nq=nq, li=li)


def _prompt_attn_step(i, qt_ref, k_ref, vt_ref, lq1, lk1, lq2, lk2, sg_ref, o_ref, *, tq, nq, li, extra=None):
    qz = _split_maps(qt_ref[...], 0)
    lam = _diff_lambda(lq1, lk1, lq2, lk2, li)

    qc = QC_ATTN
    n_qc = tq // qc

    def step(j, state, masked):
        out = []
        for m in range(2):
            row = []
            for ci in range(n_qc):
                n_keys = (ci + 1) * qc if masked else tq
                kb = k_ref[j * tq:j * tq + n_keys, :]
                vt = vt_ref[j, :, 0:n_keys]
                s = jnp.dot(kb, qz[m][:, ci * qc:(ci + 1) * qc], preferred_element_type=F32)
                if masked:
                    key = lax.broadcasted_iota(jnp.int32, s.shape, 0)
                    qry = lax.broadcasted_iota(jnp.int32, s.shape, 1) + ci * qc
                    s = jnp.where(key <= qry, s, -jnp.inf)
                m_old, l_old, a_old = state[m][ci]
                m_new = jnp.max(s, axis=0, keepdims=True)
                if m_old is not None:
                    m_new = jnp.maximum(m_old, m_new)
                p = jnp.exp2(s - m_new)
                l_new = jnp.sum(p, axis=0, keepdims=True)
                a_new = jnp.dot(vt, p.astype(BF16), preferred_element_type=F32)
                if m_old is not None:
                    alpha = jnp.exp2(m_old - m_new)
                    l_new = alpha * l_old + l_new
                    a_new = alpha * a_old + a_new
                row.append((m_new, l_new, a_new))
            out.append(row)
        return out

    for c in range(nq):
        @pl.when(i == c)
        def _():
            if extra is not None:
                extra()
            state = [[(None, None, None)] * n_qc] * 2
            for j in range(c + 1):
                state = step(j, state, j == c)
            norm = lambda row: jnp.concatenate([a / l for _, l, a in row], axis=1)
            o = norm(state[0]) - lam * norm(state[1])
            r = _rms(o, sg_ref[...], SUBLN_EPS, axis=0) * (1.0 - li)
            o_ref[...] = r.T.astype(BF16)


def _attn_prompt(qt, kb, vt, lam_params, sg_col, li, batch, seq):
    tq = TQ_ATTN
    nq = seq // tq
    q_blk = pl.BlockSpec((None, None, LANES, tq), lambda b, h, i: (h, b * nq + i, 0, 0))
    k_blk = pl.BlockSpec((seq, LANES), lambda b, h, i: (b, h))
    v_blk = pl.BlockSpec((None, nq, LANES, tq), lambda b, h, i: (h, b, 0, 0))
    return pl.pallas_call(
        functools.partial(_attn_prompt_body, tq=tq, nq=nq, li=li),
        grid=(batch, H_A, nq),
        in_specs=[q_blk, k_blk, v_blk] + [_full((1, HD_A))] * 4 + [_full((2 * HD_A, 1))],
        out_specs=pl.BlockSpec((tq, LANES), lambda b, h, i: (b * nq + i, h)),
        out_shape=jax.ShapeDtypeStruct((batch * seq, W_A), BF16),
        compiler_params=_cparams(("parallel", "parallel", "arbitrary")),
        name="attn_prompt",
    )(qt, kb, vt, *lam_params, sg_col)


def _attn_sample_body(*refs, **params):
    n_chunks = pl.num_programs(1)
    t = pl.program_id(0) * n_chunks + pl.program_id(1)
    _sample_attn_step(t, n_chunks, pl.num_programs(0) * n_chunks, *refs, **params)


def _sample_attn_step(t, n_chunks, n_steps, pt_ref, q_ref, kn_ref, vn_ref, ck_hbm, cv_hbm, lq1, lk1, lq2, lk2,
                      sg_ref, o_ref, kbuf, vbuf, sem, m_sc, l_sc, a_sc, *, layer, n_pages_step, n_tok, li,
                      run_pages=None):
    c = lax.rem(t, n_chunks)
    nq = n_tok * H_A
    ahead = PAGE_BUFFERS - 1

    def page_copies(step, slot):
        seq = lax.div(step, n_chunks)
        first = lax.rem(step, n_chunks) * n_pages_step
        out = []
        for j in range(n_pages_step):
            page = pt_ref[seq, first + j]
            out.append(pltpu.make_async_copy(ck_hbm.at[layer, page], kbuf.at[slot, j], sem.at[slot, 0]))
            out.append(pltpu.make_async_copy(cv_hbm.at[layer, page], vbuf.at[slot, j], sem.at[slot, 1]))
        return out

    def fetch(step):
        for cp in page_copies(step, lax.rem(step, PAGE_BUFFERS)):
            cp.start()

    @pl.when(t == 0)
    def _():
        for d in range(ahead):
            @pl.when(d < n_steps)
            def _():
                fetch(jnp.int32(d))

    @pl.when(t + ahead < n_steps)
    def _():
        fetch(t + ahead)

    slot = lax.rem(t, PAGE_BUFFERS)
    pltpu.make_async_copy(ck_hbm.at[layer, pl.ds(0, n_pages_step)], kbuf.at[slot], sem.at[slot, 0]).wait()
    pltpu.make_async_copy(cv_hbm.at[layer, pl.ds(0, n_pages_step)], vbuf.at[slot], sem.at[slot, 1]).wait()

    @pl.when(c == 0)
    def _():
        m_sc[...] = jnp.full(m_sc.shape, -jnp.inf, F32)
        l_sc[...] = jnp.zeros(l_sc.shape, F32)
        a_sc[...] = jnp.zeros(a_sc.shape, F32)

    q1, q2 = _split_maps(q_ref[...], 1)
    qb = jnp.concatenate([q1, q2], axis=0)

    def accumulate(blocks):
        ss = []
        for k, _, mask in blocks:
            s = lax.dot_general(qb, k.astype(BF16), (((1,), (1,)), ((), ())), preferred_element_type=F32)
            ss.append(jnp.where(mask, s, -jnp.inf))
        m_old = m_sc[...]
        m_new = m_old
        for s in ss:
            m_new = jnp.maximum(m_new, jnp.max(s, axis=-1, keepdims=True))
        alpha = jnp.exp2(m_old - m_new)
        l_new = alpha * l_sc[...]
        a_new = alpha * a_sc[...]
        for s, (_, v, _) in zip(ss, blocks):
            p = jnp.exp2(s - m_new)
            l_new = l_new + jnp.sum(p, axis=-1, keepdims=True)
            a_new = a_new + jnp.dot(p.astype(BF16), v.astype(BF16), preferred_element_type=F32)
        m_sc[...] = m_new
        l_sc[...] = l_new
        a_sc[...] = a_new

    n_rows = PAGE_SIZE * H_A
    r = lax.broadcasted_iota(jnp.int32, (2 * nq, n_rows), 0)
    col = lax.broadcasted_iota(jnp.int32, (2 * nq, n_rows), 1)
    head_of = lambda t: t & (H_A - 1)
    head_mask = head_of(r) == head_of(col)

    def page_update():
        accumulate([(kbuf[slot, j], vbuf[slot, j], head_mask) for j in range(n_pages_step)])

    if run_pages is None:
        page_update()
    else:
        run_pages(page_update)

    @pl.when(c == n_chunks - 1)
    def _():
        rn = lax.broadcasted_iota(jnp.int32, (2 * nq, nq), 0)
        cn = lax.broadcasted_iota(jnp.int32, (2 * nq, nq), 1)
        q_row = jnp.where(rn >= nq, rn - nq, rn)
        new_mask = (head_of(rn) == head_of(cn)) & (cn <= q_row)
        accumulate([(kn_ref[...], vn_ref[...], new_mask)])
        lam = _diff_lambda(lq1, lk1, lq2, lk2, li)
        o = a_sc[...] / l_sc[...]
        o_ref[...] = _rms(o[:nq] - lam * o[nq:], sg_ref[...], SUBLN_EPS) * (1.0 - li)


def _attn_sample(q, k_new, v_new, cache_k, cache_v, page_table, layer, lam_params, sg, li, n_tok):
    n_seq, n_pages = page_table.shape
    pps = math.gcd(PAGES_PER_STEP, n_pages)
    nq = n_tok * H_A
    n_rows = PAGE_SIZE * H_A

    seq_blk = pl.BlockSpec((None, nq, LANES), lambda s, c, pt: (s, 0, 0))
    const = lambda shape: pl.BlockSpec(shape, lambda s, c, pt: (0,) * len(shape))
    page_buf = pltpu.VMEM((PAGE_BUFFERS, pps, n_rows, LANES), F32)
    grid_spec = pltpu.PrefetchScalarGridSpec(
        num_scalar_prefetch=1,
        grid=(n_seq, n_pages // pps),
        in_specs=[seq_blk, seq_blk, seq_blk, _ANY, _ANY] + [const((1, HD_A))] * 4 + [const((1, 2 * HD_A))],
        out_specs=seq_blk,
        scratch_shapes=[page_buf, page_buf, pltpu.SemaphoreType.DMA((PAGE_BUFFERS, 2)),
                        pltpu.VMEM((2 * nq, 1), F32), pltpu.VMEM((2 * nq, 1), F32), pltpu.VMEM((2 * nq, LANES), F32)],
    )
    return pl.pallas_call(
        functools.partial(_attn_sample_body, layer=layer, n_pages_step=pps, n_tok=n_tok, li=li),
        grid_spec=grid_spec,
        out_shape=jax.ShapeDtypeStruct((n_seq, nq, LANES), F32),
        compiler_params=_cparams(("arbitrary", "arbitrary")),
        name="attn_sample",
    )(page_table, q, k_new, v_new, cache_k, cache_v, *lam_params, sg)


def _attn_fused_body(pt_ref, qt_ref, k_ref, vt_ref, sgc_ref, op_ref, q_ref, kn_ref, vn_ref, ck_hbm, cv_hbm,
                     lq1, lk1, lq2, lk2, sgr_ref, os_ref, kbuf, vbuf, sem, m_sc, l_sc, a_sc,
                     *, n_chunks, tq, nq, li, **sample_params):
    i = pl.program_id(2)
    t = (pl.program_id(0) * pl.num_programs(1) + pl.program_id(1)) * nq + i
    n_steps = pl.num_programs(0) * pl.num_programs(1) * nq
    lam_refs = (lq1, lk1, lq2, lk2)

    def prompt_step(page_update):
        _prompt_attn_step(i, qt_ref, k_ref, vt_ref, *lam_refs, sgc_ref, op_ref, tq=tq, nq=nq, li=li,
                          extra=page_update)

    _sample_attn_step(t, n_chunks, n_steps, pt_ref, q_ref, kn_ref, vn_ref, ck_hbm, cv_hbm, *lam_refs, sgr_ref, os_ref,
                      kbuf, vbuf, sem, m_sc, l_sc, a_sc, li=li, run_pages=prompt_step, **sample_params)


def _attn_fused(qt, kb, vt, q, k_new, v_new, cache_k, cache_v, page_table, layer, lam_params, sg, sg_col, li,
                batch, seq, n_tok, pps):
    n_seq, n_pages = page_table.shape
    tq = TQ_ATTN
    nq_blocks = seq // tq
    n_chunks = n_pages // pps
    assert batch * H_A * nq_blocks == n_seq * n_chunks
    nq = n_tok * H_A
    n_rows = PAGE_SIZE * H_A
    step_of = lambda b, h, i: (b * H_A + h) * nq_blocks + i
    q_blk = pl.BlockSpec((None, None, LANES, tq), lambda b, h, i, pt: (h, b * nq_blocks + i, 0, 0))
    k_blk = pl.BlockSpec((seq, LANES), lambda b, h, i, pt: (b, h))
    v_blk = pl.BlockSpec((None, nq_blocks, LANES, tq), lambda b, h, i, pt: (h, b, 0, 0))
    o_blk = pl.BlockSpec((tq, LANES), lambda b, h, i, pt: (b * nq_blocks + i, h))
    seq_blk = pl.BlockSpec((None, nq, LANES), lambda b, h, i, pt: (step_of(b, h, i) // n_chunks, 0, 0))
    const = lambda shape: pl.BlockSpec(shape, lambda b, h, i, pt: (0,) * len(shape))
    page_buf = pltpu.VMEM((PAGE_BUFFERS, pps, n_rows, LANES), F32)
    grid_spec = pltpu.PrefetchScalarGridSpec(
        num_scalar_prefetch=1,
        grid=(batch, H_A, nq_blocks),
        in_specs=[q_blk, k_blk, v_blk, const((2 * HD_A, 1))]
        + [seq_blk, seq_blk, seq_blk, _ANY, _ANY] + [const((1, HD_A))] * 4 + [const((1, 2 * HD_A))],
        out_specs=[o_blk, seq_blk],
        scratch_shapes=[page_buf, page_buf, pltpu.SemaphoreType.DMA((PAGE_BUFFERS, 2)),
                        pltpu.VMEM((2 * nq, 1), F32), pltpu.VMEM((2 * nq, 1), F32), pltpu.VMEM((2 * nq, LANES), F32)],
    )

    def body(pt_ref, qt_ref, k_ref, vt_ref, sgc_ref, q_ref, kn_ref, vn_ref, ck_hbm, cv_hbm, lq1, lk1, lq2, lk2,
             sgr_ref, op_ref, os_ref, *scratch):
        _attn_fused_body(pt_ref, qt_ref, k_ref, vt_ref, sgc_ref, op_ref, q_ref, kn_ref, vn_ref, ck_hbm, cv_hbm,
                         lq1, lk1, lq2, lk2, sgr_ref, os_ref, *scratch,
                         n_chunks=n_chunks, tq=tq, nq=nq_blocks, li=li, layer=layer, n_pages_step=pps, n_tok=n_tok)

    return pl.pallas_call(
        body,
        grid_spec=grid_spec,
        out_shape=[jax.ShapeDtypeStruct((batch * seq, W_A), BF16), jax.ShapeDtypeStruct((n_seq, nq, LANES), F32)],
        compiler_params=_cparams(("arbitrary", "arbitrary", "arbitrary")),
        name="attn_fused",
    )(page_table, qt, kb, vt, sg_col, q, k_new, v_new, cache_k, cache_v, *lam_params, sg)


def _group_select(vals):
    lane = lax.broadcasted_iota(jnp.int32, vals[0].shape, 1)
    out = vals[-1]
    for g in range(len(vals) - 2, -1, -1):
        out = jnp.where(lane < (g + 1) * CG_B, vals[g], out)
    return out


def _router(xn, wr_hi, wr_lo, br):
    hi = xn.astype(BF16)
    lo = (xn - hi.astype(F32)).astype(BF16)
    logits = (jnp.dot(hi, wr_hi[...], preferred_element_type=F32)
              + jnp.dot(lo, wr_hi[...], preferred_element_type=F32)
              + jnp.dot(hi, wr_lo[...], preferred_element_type=F32)) + br[...]
    lane = lax.broadcasted_iota(jnp.int32, logits.shape, 1).astype(F32)
    neg = -jnp.inf
    big = float(LANES)

    def first_argmax(vals):
        top = jnp.max(vals, axis=-1, keepdims=True)
        idx = jnp.min(jnp.where(vals == top, lane, big), axis=-1, keepdims=True)
        return top, idx

    is_group = lane < N_GROUPS
    g_top, g_idx = first_argmax(jnp.where(is_group, logits, neg))
    g_w = 1.0 / jnp.sum(jnp.where(is_group, jnp.exp(logits - g_top), 0.0), axis=-1, keepdims=True)
    e_base = N_GROUPS + EXP_PER_GROUP * g_idx
    in_group = (lane >= e_base) & (lane < e_base + EXP_PER_GROUP)
    e_vals = jnp.where(in_group, logits, neg)
    e1, i1 = first_argmax(e_vals)
    e2, i2 = first_argmax(jnp.where(lane == i1, neg, e_vals))
    t = jnp.exp(e2 - e1)
    w1 = g_w / (1.0 + t)
    w2 = g_w * t / (1.0 + t)
    a1 = i1 - e_base
    a2 = i2 - e_base
    lo_id = jnp.minimum(a1, a2)
    hi_id = jnp.maximum(a1, a2)
    w_lo = jnp.where(a1 < a2, w1, w2)
    w_hi = jnp.where(a1 < a2, w2, w1)
    pair_base = jnp.where(lo_id == 0.0, 0.0, jnp.where(lo_id == 1.0, 3.0, 5.0))
    bucket = g_idx * N_PAIRS + pair_base + (hi_id - lo_id - 1.0)
    return jnp.where(lane == 0.0, bucket, jnp.where(lane == 1.0, w_lo, jnp.where(lane == 2.0, w_hi, 0.0)))


def _mix_tail(a, b_out, c_out, x, wout_ref, gffn_ref, wr_hi, wr_lo, br, y_ref, xg_ref, rt_ref, cnt_ref, *, exact):
    if exact:
        mm = lambda t, lo, hi: jnp.dot(t, wout_ref[lo:hi, :], preferred_element_type=F32, precision=HIGHEST)
    else:
        mm = lambda t, lo, hi: jnp.dot(t.astype(BF16), wout_ref[lo:hi, :], preferred_element_type=F32)
    y = x + (mm(a, 0, W_A) + mm(b_out, W_A, W_A + W_B) + mm(c_out, W_A + W_B, D_MODEL))
    y_ref[...] = y
    xn = _rms(y, gffn_ref[...], NORM_EPS)
    route = _router(xn, wr_hi, wr_lo, br)
    xg_ref[:, 0:D_MODEL] = xn
    xg_ref[:, D_MODEL:] = route
    rt_ref[...] = route.T[0:ROUTE_FIELDS, :]
    lane = lax.broadcasted_iota(jnp.int32, route.shape, 1).astype(F32)
    hits = jnp.sum(jnp.where(lane == route[:, 0:1], 1.0, 0.0), axis=0, keepdims=True)

    @pl.when(pl.program_id(0) == 0)
    def _():
        cnt_ref[...] = jnp.zeros(cnt_ref.shape, F32)

    cnt_ref[...] += hits


def _sgu_norm(gv, sgu_g, sgu_b):
    gg = _gelu(gv)
    mu = jnp.mean(gg, axis=-1, keepdims=True)
    d = gg - mu
    var = jnp.mean(d * d, axis=-1, keepdims=True)
    return d * lax.rsqrt(var + NORM_EPS) * sgu_g + sgu_b


def _mix_prompt_body(a_ref, p_ref, halo_ref, u_ref, gv_ref, x_ref, wpool_ref, pscale_ref, sgug_ref, sgub_ref,
                     ws_ref, bs_ref, wout_ref, gffn_ref, wr_hi, wr_lo, br,
                     y_ref, xg_ref, rt_ref, cnt_ref, ext_sc, *, tm, tiles_per_seq):
    i = pl.program_id(0)
    t_in_seq = i % tiles_per_seq
    p = p_ref[...]
    ext_sc[0:POOL_HALO, :] = jnp.where(t_in_seq == 0, 0.0, halo_ref[...])
    ext_sc[POOL_HALO:, :] = p
    pos = t_in_seq * tm + lax.broadcasted_iota(jnp.int32, (tm, 1), 0)
    acc = p
    means = []
    for k in range(1, max(POOL_WINDOWS)):
        acc = acc + ext_sc[POOL_HALO - k:POOL_HALO - k + tm, :]
        if k + 1 in POOL_WINDOWS:
            cnt = jnp.minimum(pos + 1, k + 1).astype(F32)
            means.append(acc / cnt)
    d = (_group_select(means) - p).astype(BF16)
    b_out = jnp.dot(d, wpool_ref[...], preferred_element_type=F32) * pscale_ref[...]

    vn = _sgu_norm(gv_ref[...], sgug_ref[...], sgub_ref[...]).astype(BF16)
    r = lax.broadcasted_iota(jnp.int32, (CHUNK, CHUNK), 0)
    c = lax.broadcasted_iota(jnp.int32, (CHUNK, CHUNK), 1)
    w_tril = [jnp.where(r >= c, ws_ref[h], 0.0).astype(BF16) for h in range(H_C)]
    chunks = []
    for ci in range(tm // CHUNK):
        vc = vn[ci * CHUNK:(ci + 1) * CHUNK, :]
        per_head = [jnp.dot(w_tril[h], vc, preferred_element_type=F32) for h in range(H_C)]
        chunks.append(_group_select(per_head) + bs_ref[...])
    mixed = jnp.concatenate(chunks, axis=0)
    c_out = _gelu(u_ref[...]) * mixed

    _mix_tail(a_ref[...], b_out, c_out, x_ref[...], wout_ref, gffn_ref, wr_hi, wr_lo, br,
              y_ref, xg_ref, rt_ref, cnt_ref, exact=False)


def _mix_out_specs(n, tm):
    row = lambda w: pl.BlockSpec((tm, w), lambda i: (i, 0))
    specs = [row(D_MODEL), row(D_ROUTED), pl.BlockSpec((ROUTE_FIELDS, tm), lambda i: (0, i)), _full((1, LANES))]
    shapes = [jax.ShapeDtypeStruct((n, D_MODEL), F32), jax.ShapeDtypeStruct((n, D_ROUTED), F32),
              jax.ShapeDtypeStruct((ROUTE_FIELDS, n), F32), jax.ShapeDtypeStruct((1, LANES), F32)]
    return specs, shapes


def _mix_prompt(a, p, u, gv, x, lw, seq):
    n = x.shape[0]
    tm = TM_MIX
    tiles_per_seq = seq // tm
    halo_per_tile = tm // POOL_HALO
    row = lambda w: pl.BlockSpec((tm, w), lambda i: (i, 0))
    halo = pl.BlockSpec((POOL_HALO, W_B), lambda i: (jnp.maximum(i * halo_per_tile - 1, 0), 0))
    out_specs, out_shape = _mix_out_specs(n, tm)
    return pl.pallas_call(
        functools.partial(_mix_prompt_body, tm=tm, tiles_per_seq=tiles_per_seq),
        grid=(n // tm,),
        in_specs=[row(W_A), row(W_B), halo, row(W_C), row(W_C), row(D_MODEL),
                  _full((W_B, W_B)), _full((1, W_B)), _full((1, W_C)), _full((1, W_C)),
                  _full((H_C, CHUNK, CHUNK)), _full((CHUNK, W_C)), _full((D_MODEL, D_MODEL)),
                  _full((1, D_MODEL)), _full((D_MODEL, LANES)), _full((D_MODEL, LANES)), _full((1, LANES))],
        out_specs=out_specs,
        out_shape=out_shape,
        scratch_shapes=[pltpu.VMEM((POOL_HALO + tm, W_B), F32)],
        compiler_params=_cparams(("arbitrary",)),
        name="mix_prompt",
    )(a, p, p, u, gv, x, lw["wpool"], lw["pscale"], lw["sgu_g"], lw["sgu_b"], lw["w_s"], lw["bs_chunk"],
      lw["w_out"], lw["g_ffn"], lw["wr_hi"], lw["wr_lo"], lw["br"])


def _mix_sample_body(a_ref, pext_ref, u_ref, gv_ref, x_ref, wpool_ref, pscale_ref, sgug_ref, sgub_ref,
                     wl_ref, bl_ref, wout_ref, gffn_ref, wr_hi, wr_lo, br,
                     y_ref, xg_ref, rt_ref, cnt_ref, vn_ref, *, n_tok, n_seq):
    ds = []
    for i in range(n_tok):
        cur = pext_ref[POOL_BUF + i]
        acc = cur
        means = []
        for k in range(1, max(POOL_WINDOWS)):
            acc = acc + pext_ref[POOL_BUF + i - k]
            if k + 1 in POOL_WINDOWS:
                means.append(acc / float(k + 1))
        ds.append(_group_select(means) - cur)
    d = jnp.concatenate(ds, axis=0)
    b_out = jnp.dot(d, wpool_ref[...], preferred_element_type=F32, precision=HIGHEST) * pscale_ref[...]

    vn = _sgu_norm(gv_ref[...], sgug_ref[...], sgub_ref[...])
    vn_ref[...] = vn
    rows = []
    for i in range(n_tok):
        acc = jnp.broadcast_to(bl_ref[i:i + 1, :], (n_seq, W_C))
        for j in range(i + 1):
            acc = acc + wl_ref[i * n_tok + j:i * n_tok + j + 1, :] * vn[j * n_seq:(j + 1) * n_seq, :]
        rows.append(acc)
    c_out = _gelu(u_ref[...]) * jnp.concatenate(rows, axis=0)

    _mix_tail(a_ref[...], b_out, c_out, x_ref[...], wout_ref, gffn_ref, wr_hi, wr_lo, br,
              y_ref, xg_ref, rt_ref, cnt_ref, exact=True)


def _mix_sample(a, pext, u, gv, x, lw, n_tok, n_seq):
    n = x.shape[0]
    out_specs, out_shape = _mix_out_specs(n, n)
    return pl.pallas_call(
        functools.partial(_mix_sample_body, n_tok=n_tok, n_seq=n_seq),
        grid=(1,),
        in_specs=[_full((n, W_A)), _full((POOL_BUF + n_tok, n_seq, W_B)), _full((n, W_C)), _full((n, W_C)),
                  _full((n, D_MODEL)), _full((W_B, W_B)), _full((1, W_B)), _full((1, W_C)), _full((1, W_C)),
                  _full((n_tok * n_tok, W_C)), _full((n_tok, W_C)), _full((D_MODEL, D_MODEL)),
                  _full((1, D_MODEL)), _full((D_MODEL, LANES)), _full((D_MODEL, LANES)), _full((1, LANES))],
        out_specs=out_specs + [_full((n, W_C))],
        out_shape=out_shape + [jax.ShapeDtypeStruct((n, W_C), F32)],
        compiler_params=_cparams(("arbitrary",)),
        name="mix_sample",
    )(a, pext, u, gv, x, lw["wpool_f32"], lw["pscale"], lw["sgu_g"], lw["sgu_b"], lw["wl_new"], lw["bl_new"],
      lw["w_out_f32"], lw["g_ffn"], lw["wr_hi"], lw["wr_lo"], lw["br"])


def _moe_body(src_ref, elo_ref, ehi_ref, nv_ref, nact_ref, x_hbm, wg_lo, wg_hi, wu_lo, wu_hi, wd_lo, wd_hi,
              y_hbm, xbuf, ybuf, gsem, ssem, *, tm):
    i = pl.program_id(0)
    n_act = nact_ref[0]
    slot = lax.rem(i, 2)

    def row_copy_in(tile, s, r):
        return pltpu.make_async_copy(x_hbm.at[pl.ds(src_ref[tile * tm + r], 1)], xbuf.at[s, pl.ds(r, 1)], gsem.at[s])

    def row_copy_out(tile, s, r):
        return pltpu.make_async_copy(ybuf.at[s, pl.ds(r, 1)], y_hbm.at[pl.ds(src_ref[tile * tm + r], 1)], ssem.at[s])

    def start_gather(tile, s, unrolled):
        if unrolled:
            for r in range(tm):
                row_copy_in(tile, s, r).start(priority=r % 2)
            return

        def one(r, c):
            row_copy_in(tile, s, r).start()
            return c
        lax.fori_loop(0, tm, one, 0, unroll=8)

    def wait_gather(s):
        pltpu.make_async_copy(x_hbm.at[pl.ds(0, tm)], xbuf.at[s], gsem.at[s]).wait()

    def whole_groups(n):
        return pl.multiple_of(lax.shift_left(lax.shift_right_logical(n, 3), 3), SUBLANES)

    def start_scatter(tile, s):
        n = nv_ref[tile]

        @pl.when(n == tm)
        def _():
            for r in range(tm):
                row_copy_out(tile, s, r).start(priority=r % 2)

        @pl.when(n < tm)
        def _():
            def group(g, c):
                for u in range(SUBLANES):
                    row_copy_out(tile, s, g * SUBLANES + u).start()
                return c
            lax.fori_loop(0, lax.shift_right_logical(n, 3), group, 0)

            def one(r, c):
                row_copy_out(tile, s, r).start()
                return c
            lax.fori_loop(whole_groups(n), n, one, 0)

    def wait_scatter(tile, s):
        n = nv_ref[tile]
        n_blk = whole_groups(n)

        @pl.when(n_blk > 0)
        def _():
            pltpu.make_async_copy(ybuf.at[s, pl.ds(0, n_blk)], y_hbm.at[pl.ds(0, n_blk)], ssem.at[s]).wait()

        def one(r, c):
            pltpu.make_async_copy(ybuf.at[s, pl.ds(0, 1)], y_hbm.at[pl.ds(0, 1)], ssem.at[s]).wait()
            return c
        lax.fori_loop(n_blk, n, one, 0)

    @pl.when(i == 0)
    def _():
        start_gather(0, 0, False)

    @pl.when(i + 1 < n_act)
    def _():
        start_gather(i + 1, 1 - slot, True)

    @pl.when(i < n_act)
    def _():
        wait_gather(slot)

        @pl.when(i >= 2)
        def _():
            wait_scatter(i - 2, slot)

        xrow = xbuf[slot]
        x = xrow[:, 0:D_MODEL].astype(BF16)
        gate_lo = xrow[:, D_MODEL + 1:D_MODEL + 2]
        gate_hi = xrow[:, D_MODEL + 2:D_MODEL + 3]

        def hidden(wg, wu, gate):
            hg = jnp.dot(x, wg[...], preferred_element_type=F32)
            hu = jnp.dot(x, wu[...], preferred_element_type=F32)
            return (hg / (1.0 + jnp.exp(-hg)) * hu * gate).astype(BF16)

        h_lo = hidden(wg_lo, wu_lo, gate_lo)
        h_hi = hidden(wg_hi, wu_hi, gate_hi)
        ybuf[slot] = (jnp.dot(h_lo, wd_lo[...], preferred_element_type=F32)
                      + jnp.dot(h_hi, wd_hi[...], preferred_element_type=F32))
        start_scatter(i, slot)

        @pl.when(i == n_act - 1)
        def _():
            wait_scatter(i, slot)

            @pl.when(i >= 1)
            def _():
                wait_scatter(i - 1, 1 - slot)


def _moe(xg, src, e_lo, e_hi, n_valid, n_active, wg, wu, wd, tm):
    n = xg.shape[0]
    n_tiles = e_lo.shape[0]
    wspec = lambda shape, which: pl.BlockSpec(
        (None,) + shape, lambda i, src, elo, ehi, nv, na: ((elo, ehi)[which][i], 0, 0))
    up, dn = (D_MODEL, D_FF), (D_FF, D_MODEL)
    grid_spec = pltpu.PrefetchScalarGridSpec(
        num_scalar_prefetch=5,
        grid=(n_tiles,),
        in_specs=[_ANY, wspec(up, 0), wspec(up, 1), wspec(up, 0), wspec(up, 1), wspec(dn, 0), wspec(dn, 1)],
        out_specs=_ANY,
        scratch_shapes=[pltpu.VMEM((2, tm, D_ROUTED), F32), pltpu.VMEM((2, tm, D_MODEL), F32),
                        pltpu.SemaphoreType.DMA((2,)), pltpu.SemaphoreType.DMA((2,))],
    )
    return pl.pallas_call(
        functools.partial(_moe_body, tm=tm),
        grid_spec=grid_spec,
        out_shape=jax.ShapeDtypeStruct((n, D_MODEL), F32),
        compiler_params=_cparams(("arbitrary",)),
        name="moe",
    )(src, e_lo, e_hi, n_valid, n_active, xg, wg, wg, wu, wu, wd, wd)


_PAIR_LO = (0, 0, 0, 1, 1, 2)
_PAIR_HI = (1, 2, 3, 2, 3, 3)


def _moe_dispatch(xg, route_t, cnt, wg, wu, wd, tm):
    n = xg.shape[0]
    n_tiles = -(-n // tm) + N_BUCKETS
    bucket = route_t[0].astype(jnp.int32)
    order = jnp.argsort(bucket, stable=True).astype(jnp.int32)
    counts = cnt[0, :N_BUCKETS].astype(jnp.int32)
    padded = ((counts + tm - 1) // tm) * tm
    ends = jnp.cumsum(padded)
    starts = ends - padded
    sorted_starts = jnp.cumsum(counts) - counts
    tile_start = jnp.arange(n_tiles, dtype=jnp.int32) * tm
    tile_bucket = jnp.minimum(jnp.sum(tile_start[:, None] >= ends[None, :], axis=1, dtype=jnp.int32), N_BUCKETS - 1)
    in_bucket = tile_start - starts[tile_bucket]
    n_valid = jnp.where(tile_start < ends[-1], jnp.clip(counts[tile_bucket] - in_bucket, 0, tm), 0).astype(jnp.int32)
    r_in = jnp.arange(tm, dtype=jnp.int32)[None, :]
    sorted_idx = (sorted_starts[tile_bucket] + in_bucket)[:, None] + r_in
    src = jnp.where(r_in < n_valid[:, None], order[jnp.clip(sorted_idx, 0, n - 1)], 0).reshape(-1)
    group = tile_bucket // N_PAIRS
    pair = tile_bucket % N_PAIRS
    e_lo = group * EXP_PER_GROUP + jnp.asarray(_PAIR_LO, jnp.int32)[pair]
    e_hi = group * EXP_PER_GROUP + jnp.asarray(_PAIR_HI, jnp.int32)[pair]
    n_active = (ends[-1] // tm).reshape(1).astype(jnp.int32)
    return _moe(xg, src.astype(jnp.int32), e_lo, e_hi, n_valid, n_active, wg, wu, wd, tm)


def _final_body(x_ref, y_ref, g_ref, o_ref):
    o_ref[...] = _rms(x_ref[...] + y_ref[...], g_ref[...], NORM_EPS)


def _final_norm(x, y_moe, g, tm):
    n = x.shape[0]
    row = pl.BlockSpec((tm, D_MODEL), lambda i: (i, 0))
    return pl.pallas_call(
        _final_body,
        grid=(n // tm,),
        in_specs=[row, row, _full((1, D_MODEL))],
        out_specs=row,
        out_shape=jax.ShapeDtypeStruct((n, D_MODEL), F32),
        compiler_params=_cparams(("parallel",)),
        name="final_norm",
    )(x, y_moe, g.reshape(1, D_MODEL))


def _layer_weights(l, n_tok, norm_ffn_g, w_out, subln_g, w_pool, pool_scale, sgu_norm_g, sgu_norm_b,
                   w_spatial, b_spatial, w_router_group, b_router_group, w_router_expert, b_router_expert):
    wpool = jnp.zeros((W_B, W_B), F32)
    for g in range(len(POOL_WINDOWS)):
        wpool = wpool.at[g * CG_B:(g + 1) * CG_B, g * CG_B:(g + 1) * CG_B].set(w_pool[l, g])
    wr = jnp.concatenate([w_router_group[l], w_router_expert[l],
                          jnp.zeros((D_MODEL, LANES - N_GROUPS - N_EXPERTS), F32)], axis=1)
    wr_hi = wr.astype(BF16)
    br = jnp.concatenate([b_router_group[l], b_router_expert[l],
                          jnp.zeros((LANES - N_GROUPS - N_EXPERTS,), F32)]).reshape(1, LANES)
    per_lane = lambda t: jnp.repeat(t, HD_C, axis=-1)
    ws_new = jnp.tril(w_spatial[l, :, :n_tok, :n_tok])
    return dict(
        wpool=wpool.astype(BF16), wpool_f32=wpool, pscale=pool_scale[l].reshape(1, W_B),
        sgu_g=sgu_norm_g[l].reshape(1, W_C), sgu_b=sgu_norm_b[l].reshape(1, W_C),
        w_s=w_spatial[l], bs_chunk=per_lane(b_spatial[l].T),
        wl_new=per_lane(jnp.transpose(ws_new, (1, 2, 0)).reshape(n_tok * n_tok, H_C)),
        bl_new=per_lane(b_spatial[l, :, :n_tok].T),
        w_out=w_out[l].astype(BF16), w_out_f32=w_out[l], g_ffn=norm_ffn_g[l].reshape(1, D_MODEL),
        wr_hi=wr_hi, wr_lo=(wr - wr_hi.astype(F32)).astype(BF16), br=br,
        sg=subln_g[l].reshape(1, 2 * HD_A), sg_col=subln_g[l].reshape(2 * HD_A, 1),
    )


def kernel(x_prompt, x_sample, cache_k, cache_v, state_pool, page_table, norm_mix_g, norm_ffn_g, w_in, w_out, lambda_q1, lambda_k1, lambda_q2, lambda_k2, subln_g, w_pool, pool_scale, sgu_norm_g, sgu_norm_b, w_spatial, b_spatial, w_router_group, b_router_group, w_router_expert, b_router_expert, w_exp_gate, w_exp_up, w_exp_down, final_norm_g):
    batch, seq, _ = x_prompt.shape
    n_seq, n_tok, _ = x_sample.shape
    depth = w_in.shape[0]
    n_p = batch * seq
    n_s = n_seq * n_tok
    assert seq % TM_MIX == 0 and seq % TQ_ATTN == 0 and n_p % TM_INPROJ == 0 and TM_INPROJ % TQ_ATTN == 0
    assert n_s % SUBLANES == 0 and page_table.shape[1] * PAGE_SIZE >= max(POOL_WINDOWS)

    xp = x_prompt.reshape(n_p, D_MODEL)
    xs = jnp.transpose(x_sample, (1, 0, 2)).reshape(n_s, D_MODEL)
    yp_moe = ys_moe = None
    kv_prompt = tuple(jnp.zeros((depth, n_p * H_A, 2 * HD_A), F32) for _ in range(2))
    ck = cache_k.reshape(depth, cache_k.shape[1], PAGE_SIZE * H_A, 2 * HD_A)
    cv = cache_v.reshape(depth, cache_v.shape[1], PAGE_SIZE * H_A, 2 * HD_A)

    def to_seq_major(t, width):
        return jnp.transpose(t.reshape(n_tok, n_seq, width), (1, 0, 2))

    pp_l, ks_l, vs_l, ps_l, cs_l = [], [], [], [], []
    for l in range(depth):
        li = _lambda_init(l)
        lw = _layer_weights(l, n_tok, norm_ffn_g, w_out, subln_g, w_pool, pool_scale, sgu_norm_g, sgu_norm_b,
                            w_spatial, b_spatial, w_router_group, b_router_group, w_router_expert, b_router_expert)
        lam_params = tuple(t[l].reshape(1, HD_A) for t in (lambda_q1, lambda_k1, lambda_q2, lambda_k2))
        wg, wu, wd = w_exp_gate[l].astype(BF16), w_exp_up[l].astype(BF16), w_exp_down[l].astype(BF16)

        xp, qt, k_all, v_all, kb, vt, p_p, u_p, gv_p = _inproj_prompt(
            xp, yp_moe, norm_mix_g[l], w_in[l].astype(BF16), l, depth, kv_prompt)
        kv_prompt = (k_all, v_all)
        xs, q, k, v, p, u, gv = _inproj_sample(xs, ys_moe, norm_mix_g[l], w_in[l])
        per_seq = lambda t: to_seq_major(t, W_A).reshape(n_seq, n_tok * H_A, 2 * HD_A)
        n_pages = page_table.shape[1]
        pps = math.gcd(PAGES_PER_STEP, n_pages)
        if batch * H_A * (seq // TQ_ATTN) == n_seq * (n_pages // pps):
            a_p, a = _attn_fused(qt, kb, vt, per_seq(q), per_seq(k), per_seq(v), ck, cv, page_table, l,
                                 lam_params, lw["sg"], lw["sg_col"], li, batch, seq, n_tok, pps)
        else:
            a_p = _attn_prompt(qt, kb, vt, lam_params, lw["sg_col"], li, batch, seq)
            a = _attn_sample(per_seq(q), per_seq(k), per_seq(v), ck, cv, page_table, l, lam_params, lw["sg"], li,
                             n_tok)

        xp, xg, route_t, cnt = _mix_prompt(a_p, p_p, u_p, gv_p, xp, lw, seq)
        yp_moe = _moe_dispatch(xg, route_t, cnt, wg, wu, wd, TM_MOE)
        pp_l.append(p_p.reshape(batch, seq, W_B)[:, seq - POOL_BUF:])

        a = jnp.transpose(a.reshape(n_seq, n_tok, W_A), (1, 0, 2)).reshape(n_s, W_A)
        pext = jnp.concatenate([jnp.transpose(state_pool[l], (1, 0, 2)), p.reshape(n_tok, n_seq, W_B)], axis=0)
        xs, xg, route_t, cnt, vn = _mix_sample(a, pext, u, gv, xs, lw, n_tok, n_seq)
        ys_moe = _moe_dispatch(xg, route_t, cnt, wg, wu, wd, TM_MOE_SAMPLE)
        ks_l.append(to_seq_major(k, W_A).reshape(n_seq, n_tok, H_A, 2 * HD_A))
        vs_l.append(to_seq_major(v, W_A).reshape(n_seq, n_tok, H_A, 2 * HD_A))
        ps_l.append(jnp.transpose(pext[n_tok:], (1, 0, 2)))
        cs_l.append(to_seq_major(vn, W_C))

    y_prompt = _final_norm(xp, yp_moe, final_norm_g, TM_FINAL).reshape(batch, seq, D_MODEL)
    y_sample = to_seq_major(_final_norm(xs, ys_moe, final_norm_g, n_s), D_MODEL)
    k_prompt = kv_prompt[0].reshape(depth, batch, seq, H_A, 2 * HD_A)
    v_prompt = kv_prompt[1].reshape(depth, batch, seq, H_A, 2 * HD_A)
    return (y_prompt, y_sample, k_prompt, v_prompt, jnp.stack(pp_l),
            jnp.stack(ks_l), jnp.stack(vs_l), jnp.stack(ps_l), jnp.stack(cs_l))
```
